```python
import jax, jax.numpy as jnp
from jax import lax
import numpy as np

D_MODEL = 2048
BATCH = 8
SEQ = 2048
DEPTH = 2

N_MIXERS = 2
MLA_HEADS = 16
Q_LORA = 512
KV_LORA = 512
QK_NOPE = 128
QK_ROPE = 64
V_HEAD = 128
ROPE_THETA = 10000.0
Q_BLOCK = 128
CONV_WIDTH = 31
D_FF = (7 * D_MODEL) // 2
N_EXPERTS = 8
TOP_K = 2
EPS = 1e-6
ADA_INIT = 0.5
MAX_POS_OFFSET = 1024
N_MLA = (DEPTH + 1) // 2
N_CONV = DEPTH // 2
N_DENSE = (DEPTH + 1) // 2
N_MOE = DEPTH // 2

kernel_name = "hybrid_mla_conformer_moe_adaln"


def rmsnorm(x, g):
    xf = x.astype(jnp.float32)
    y = xf * lax.rsqrt(jnp.mean(xf * xf, axis=-1, keepdims=True) + EPS)
    return (y * g.astype(jnp.float32)).astype(x.dtype)


def layernorm(x, g, b):
    xf = x.astype(jnp.float32)
    mu = jnp.mean(xf, axis=-1, keepdims=True)
    var = jnp.mean(jnp.square(xf - mu), axis=-1, keepdims=True)
    y = (xf - mu) * lax.rsqrt(var + EPS)
    return (y * g.astype(jnp.float32) + b.astype(jnp.float32)).astype(x.dtype)


def rope(x, cos, sin):
    half = x.shape[-1] // 2
    xf = x.astype(jnp.float32)
    x1, x2 = xf[..., :half], xf[..., half:]
    out = jnp.concatenate([x1 * cos - x2 * sin, x1 * sin + x2 * cos], axis=-1)
    return out.astype(x.dtype)


def adaln(x, g, silu_c, w_ada, b_ada):
    mod = silu_c @ w_ada + b_ada
    shift, scale, gate = jnp.split(mod, 3, axis=-1)
    h = rmsnorm(x, g) * (1.0 + scale[:, None, :]) + shift[:, None, :]
    return h, gate[:, None, :]


def mla(h, positions, w_in, g_q, g_kv, w_uq, w_ukv, w_out):
    B, S, _ = h.shape
    H = MLA_HEADS
    lat = h @ w_in
    c_q = rmsnorm(lat[..., :Q_LORA], g_q)
    c_kv = rmsnorm(lat[..., Q_LORA:Q_LORA + KV_LORA], g_kv)
    k_r = lat[..., Q_LORA + KV_LORA:]
    q = (c_q @ w_uq).reshape(B, S, H, QK_NOPE + QK_ROPE)
    q_n, q_r = q[..., :QK_NOPE], q[..., QK_NOPE:]
    kv = (c_kv @ w_ukv).reshape(B, S, H, QK_NOPE + V_HEAD)
    k_n, v = kv[..., :QK_NOPE], kv[..., QK_NOPE:]

    freqs = ROPE_THETA ** (-jnp.arange(0, QK_ROPE, 2, dtype=jnp.float32) / QK_ROPE)
    ang = positions.astype(jnp.float32)[..., None] * freqs
    cos, sin = jnp.cos(ang), jnp.sin(ang)
    q_r = rope(q_r, cos[:, :, None, :], sin[:, :, None, :])
    k_r = rope(k_r, cos, sin)

    scale = (QK_NOPE + QK_ROPE) ** -0.5
    nb = S // Q_BLOCK
    qn_b = q_n.reshape(B, nb, Q_BLOCK, H, QK_NOPE).transpose(1, 0, 2, 3, 4)
    qr_b = q_r.reshape(B, nb, Q_BLOCK, H, QK_ROPE).transpose(1, 0, 2, 3, 4)
    key_idx = jnp.arange(S)

    def block(args):
        qn, qr, i = args
        s = (jnp.einsum('bqhd,bkhd->bhqk', qn, k_n)
             + jnp.einsum('bqhr,bkr->bhqk', qr, k_r)).astype(jnp.float32) * scale
        q_idx = i * Q_BLOCK + jnp.arange(Q_BLOCK)
        mask = key_idx[None, :] <= q_idx[:, None]
        s = jnp.where(mask[None, None], s, -jnp.inf)
        p = jax.nn.softmax(s, axis=-1).astype(v.dtype)
        return jnp.einsum('bhqk,bkhd->bqhd', p, v)

    o = lax.map(block, (qn_b, qr_b, jnp.arange(nb)))
    o = o.transpose(1, 0, 2, 3, 4).reshape(B, S, H * V_HEAD)
    return o @ w_out


def conformer_conv(h, w_pw1, b_pw1, w_dw, b_dw, ln_g, ln_b, w_pw2, b_pw2):
    D = h.shape[-1]
    u = h @ w_pw1 + b_pw1
    a, g = jnp.split(u, 2, axis=-1)
    u = a * jax.nn.sigmoid(g)
    u = lax.conv_general_dilated(
        u, w_dw[:, None, :].astype(u.dtype), window_strides=(1,),
        padding=[(CONV_WIDTH - 1, 0)],
        dimension_numbers=('NWC', 'WIO', 'NWC'), feature_group_count=D) + b_dw
    u = jax.nn.silu(layernorm(u, ln_g, ln_b))
    return u @ w_pw2 + b_pw2


def swiglu(h, w_gate, w_up, w_down):
    return (jax.nn.silu(h @ w_gate) * (h @ w_up)) @ w_down


def moe(h, w_router, w_gate, w_up, w_down):
    B, S, D = h.shape
    t = h.reshape(B * S, D)
    logits = (t @ w_router).astype(jnp.float32)
    top_v, top_i = lax.top_k(logits, TOP_K)
    top_w = jax.nn.softmax(top_v, axis=-1)
    comb = jnp.sum(jax.nn.one_hot(top_i, N_EXPERTS, dtype=jnp.float32) * top_w[..., None],
                   axis=1).astype(h.dtype)
    y = jnp.zeros_like(t)
    for e in range(N_EXPERTS):
        y = y + comb[:, e:e + 1] * swiglu(t, w_gate[e], w_up[e], w_down[e])
    return y.reshape(B, S, D)


def setup_inputs(seed: int = 0) -> dict:
    key = jax.random.key(seed)
    ks = jax.random.split(key, 32)
    D, H, F, E = D_MODEL, MLA_HEADS, D_FF, N_EXPERTS
    f32 = jnp.float32

    def w(k, shape, fan_in, mult=1.0):
        return jax.random.normal(k, shape, f32) * (mult * fan_in ** -0.5)

    def gain(k, shape):
        return 1.0 + 0.02 * jax.random.normal(k, shape, f32)

    def bias(k, shape):
        return 0.02 * jax.random.normal(k, shape, f32)

    x = jax.random.normal(ks[0], (BATCH, SEQ, D), f32)
    c = jax.random.normal(ks[1], (BATCH, D), f32)
    offs = jax.random.randint(ks[2], (BATCH, 1), 0, MAX_POS_OFFSET, dtype=jnp.int32)
    positions = offs + jnp.arange(SEQ, dtype=jnp.int32)[None, :]
    return {
        'x': x, 'c': c, 'positions': positions,
        'ada_w': w(ks[3], (DEPTH, 2, D, 3 * D), D, ADA_INIT),
        'ada_b': bias(ks[4], (DEPTH, 2, 3 * D)),
        'norm_g': gain(ks[5], (DEPTH, 2, D)),
        'mla_w_in': w(ks[6], (N_MLA, D, Q_LORA + KV_LORA + QK_ROPE), D),
        'mla_g_q': gain(ks[7], (N_MLA, Q_LORA)),
        'mla_g_kv': gain(ks[8], (N_MLA, KV_LORA)),
        'mla_w_uq': w(ks[9], (N_MLA, Q_LORA, H * (QK_NOPE + QK_ROPE)), Q_LORA),
        'mla_w_ukv': w(ks[10], (N_MLA, KV_LORA, H * (QK_NOPE + V_HEAD)), KV_LORA),
        'mla_w_out': w(ks[11], (N_MLA, H * V_HEAD, D), H * V_HEAD),
        'conv_w_pw1': w(ks[12], (N_CONV, D, 2 * D), D),
        'conv_b_pw1': bias(ks[13], (N_CONV, 2 * D)),
        'conv_w_dw': w(ks[14], (N_CONV, CONV_WIDTH, D), CONV_WIDTH),
        'conv_b_dw': bias(ks[15], (N_CONV, D)),
        'conv_ln_g': gain(ks[16], (N_CONV, D)),
        'conv_ln_b': bias(ks[17], (N_CONV, D)),
        'conv_w_pw2': w(ks[18], (N_CONV, D, D), D),
        'conv_b_pw2': bias(ks[19], (N_CONV, D)),
        'ffn_w_gate': w(ks[20], (N_DENSE, D, F), D),
        'ffn_w_up': w(ks[21], (N_DENSE, D, F), D),
        'ffn_w_down': w(ks[22], (N_DENSE, F, D), F),
        'moe_w_router': w(ks[23], (N_MOE, D, E), D),
        'moe_w_gate': w(ks[24], (N_MOE, E, D, F), D),
        'moe_w_up': w(ks[25], (N_MOE, E, D, F), D),
        'moe_w_down': w(ks[26], (N_MOE, E, F, D), F),
        'final_g': gain(ks[27], (D,)),
    }


def reference(x, c, positions, ada_w, ada_b, norm_g,
              mla_w_in, mla_g_q, mla_g_kv, mla_w_uq, mla_w_ukv, mla_w_out,
              conv_w_pw1, conv_b_pw1, conv_w_dw, conv_b_dw, conv_ln_g, conv_ln_b,
              conv_w_pw2, conv_b_pw2,
              ffn_w_gate, ffn_w_up, ffn_w_down,
              moe_w_router, moe_w_gate, moe_w_up, moe_w_down,
              final_g):
    silu_c = jax.nn.silu(c)
    for i in range(DEPTH):
        j = i // N_MIXERS
        h, gate = adaln(x, norm_g[i, 0], silu_c, ada_w[i, 0], ada_b[i, 0])
        if i % N_MIXERS == 0:
            y = mla(h, positions, mla_w_in[j], mla_g_q[j], mla_g_kv[j],
                    mla_w_uq[j], mla_w_ukv[j], mla_w_out[j])
        else:
            y = conformer_conv(h, conv_w_pw1[j], conv_b_pw1[j], conv_w_dw[j], conv_b_dw[j],
                               conv_ln_g[j], conv_ln_b[j], conv_w_pw2[j], conv_b_pw2[j])
        x = x + gate * y
        h, gate = adaln(x, norm_g[i, 1], silu_c, ada_w[i, 1], ada_b[i, 1])
        k = i // 2
        if i % 2 == 0:
            y = swiglu(h, ffn_w_gate[k], ffn_w_up[k], ffn_w_down[k])
        else:
            y = moe(h, moe_w_router[k], moe_w_gate[k], moe_w_up[k], moe_w_down[k])
        x = x + gate * y
    return rmsnorm(x, final_g)
```

```python
import functools

import numpy as np
import jax
import jax.numpy as jnp
from jax import lax
from jax.experimental import pallas as pl
from jax.experimental.pallas import tpu as pltpu

F32 = jnp.float32
BF16 = jnp.bfloat16

MLA_HEADS = 16
Q_LORA = 512
KV_LORA = 512
QK_NOPE = 128
QK_ROPE = 64
V_HEAD = 128
ROPE_THETA = 10000.0
CONV_WIDTH = 31
N_EXPERTS = 8
EPS = 1e-6

LANES = 128
HEAD_PAD = 2 * LANES
HALO = 32
VMEM_LIMIT = 56 * 1024 * 1024


def _cparams(sem):
    return pltpu.CompilerParams(dimension_semantics=sem, vmem_limit_bytes=VMEM_LIMIT)


def _rms(x, g):
    return x * lax.rsqrt(jnp.mean(x * x, axis=-1, keepdims=True) + EPS) * g


def _adaln(x, g, mod, d):
    return _rms(x, g) * (1.0 + mod[:, d:2 * d]) + mod[:, 0:d]


def _dot(a, b):
    return jnp.dot(a, b, preferred_element_type=F32)


def _ada_kernel(c_ref, w_ref, b_ref, o_ref):
    c = c_ref[...]
    sc = (c * jax.nn.sigmoid(c)).astype(BF16)
    o_ref[0] = _dot(sc, w_ref[0].astype(BF16)) + b_ref[0]


def _ada_call(c, ada_w, ada_b):
    n, d, d3 = ada_w.shape
    b = c.shape[0]
    tn = 1024
    return pl.pallas_call(
        _ada_kernel,
        grid=(n, d3 // tn),
        in_specs=[
            pl.BlockSpec((b, d), lambda i, j: (0, 0)),
            pl.BlockSpec((1, d, tn), lambda i, j: (i, 0, j)),
            pl.BlockSpec((1, 1, tn), lambda i, j: (i, 0, j)),
        ],
        out_specs=pl.BlockSpec((1, b, tn), lambda i, j: (i, 0, j)),
        out_shape=jax.ShapeDtypeStruct((n, b, d3), F32),
        compiler_params=_cparams(("arbitrary", "arbitrary")),
        name="ada",
    )(c, ada_w, ada_b.reshape(n, 1, d3))


def _rope_tile(t, cosv, sa, sb):
    return t * cosv + pltpu.roll(t, 32, 1) * sa + pltpu.roll(t, 96, 1) * sb


def _mla_front_kernel(x_ref, pos_ref, mod_ref, g_ref, win_ref, gq_ref, gkv_ref,
                      wq_ref, wk_ref, wv_ref, rc_ref, q_ref, k_ref, v_ref, *, d):
    x = x_ref[0]
    h = _adaln(x, g_ref[...], mod_ref[0], d).astype(BF16)
    lat = _dot(h, win_ref[...])
    cq = _rms(lat[:, 0:Q_LORA], gq_ref[...]).astype(BF16)
    ckv = _rms(lat[:, Q_LORA:Q_LORA + KV_LORA], gkv_ref[...]).astype(BF16)
    kr = lat[:, Q_LORA + KV_LORA:Q_LORA + KV_LORA + LANES]

    ang = pos_ref[0].astype(F32) * rc_ref[0:1, :]
    cosv = jnp.cos(ang)
    sinv = jnp.sin(ang)
    sa = sinv * rc_ref[1:2, :]
    sb = sinv * rc_ref[2:3, :]

    q = _dot(cq, wq_ref[...])
    for hh in range(MLA_HEADS):
        lo = hh * HEAD_PAD
        q_ref[0, :, lo:lo + LANES] = q[:, lo:lo + LANES].astype(BF16)
        q_ref[0, :, lo + LANES:lo + HEAD_PAD] = _rope_tile(
            q[:, lo + LANES:lo + HEAD_PAD], cosv, sa, sb).astype(BF16)

    kn = _dot(ckv, wk_ref[...])
    krr = _rope_tile(kr, cosv, sa, sb).astype(BF16)
    for hh in range(MLA_HEADS):
        lo = hh * HEAD_PAD
        k_ref[0, :, lo:lo + LANES] = kn[:, hh * LANES:(hh + 1) * LANES].astype(BF16)
        k_ref[0, :, lo + LANES:lo + HEAD_PAD] = krr
    v_ref[0] = _dot(ckv, wv_ref[...]).astype(BF16)


def _mla_front_call(x, pos3, mod, g, win, gq, gkv, wq, wk, wv, rc, tm):
    b, s, d = x.shape
    hq = MLA_HEADS * HEAD_PAD
    hv = MLA_HEADS * V_HEAD
    const = lambda shape: pl.BlockSpec(shape, lambda bi, si: (0,) * len(shape))
    return pl.pallas_call(
        functools.partial(_mla_front_kernel, d=d),
        grid=(b, s // tm),
        in_specs=[
            pl.BlockSpec((1, tm, d), lambda bi, si: (bi, si, 0)),
            pl.BlockSpec((1, tm, 1), lambda bi, si: (bi, si, 0)),
            pl.BlockSpec((1, 1, 3 * d), lambda bi, si: (bi, 0, 0)),
            const(g.shape), const(win.shape), const(gq.shape), const(gkv.shape),
            const(wq.shape), const(wk.shape), const(wv.shape), const(rc.shape),
        ],
        out_specs=[
            pl.BlockSpec((1, tm, hq), lambda bi, si: (bi, si, 0)),
            pl.BlockSpec((1, tm, hq), lambda bi, si: (bi, si, 0)),
            pl.BlockSpec((1, tm, hv), lambda bi, si: (bi, si, 0)),
        ],
        out_shape=[
            jax.ShapeDtypeStruct((b, s, hq), BF16),
            jax.ShapeDtypeStruct((b, s, hq), BF16),
            jax.ShapeDtypeStruct((b, s, hv), BF16),
        ],
        compiler_params=_cparams(("arbitrary", "arbitrary")),
        name="mla_front",
    )(x, pos3, mod, g, win, gq, gkv, wq, wk, wv, rc)


def _attn_kernel(q_ref, k_ref, v_ref, o_ref, *, tq, tk):
    qi = pl.program_id(2)
    q = q_ref[0]
    row = qi * tq + lax.broadcasted_iota(jnp.int32, (tq, tk), 0)
    col = lax.broadcasted_iota(jnp.int32, (tq, tk), 1)

    def body(j, carry):
        m, l, acc = carry
        start = pl.multiple_of(j * tk, tk)
        k = k_ref[0, pl.ds(start, tk), :]
        v = v_ref[0, pl.ds(start, tk), :]
        s = lax.dot_general(q, k, (((1,), (1,)), ((), ())), preferred_element_type=F32)
        s = jnp.where(col + j * tk <= row, s, -jnp.inf)
        m_new = jnp.maximum(m, jnp.max(s, axis=-1, keepdims=True))
        alpha = jnp.exp(m - m_new)
        p = jnp.exp(s - m_new)
        l = alpha * l + jnp.sum(p, axis=-1, keepdims=True)
        acc = alpha * acc + _dot(p.astype(BF16), v)
        return m_new, l, acc

    n_kv = (qi * tq + tq + tk - 1) // tk
    init = (jnp.full((tq, 1), -jnp.inf, F32), jnp.zeros((tq, 1), F32),
            jnp.zeros((tq, V_HEAD), F32))
    _, l, acc = lax.fori_loop(0, n_kv, body, init)
    o_ref[0] = (acc / l).astype(BF16)


def _attn_call(q, k, v, tq, tk):
    b, s, _ = q.shape
    return pl.pallas_call(
        functools.partial(_attn_kernel, tq=tq, tk=tk),
        grid=(b, MLA_HEADS, s // tq),
        in_specs=[
            pl.BlockSpec((1, tq, HEAD_PAD), lambda bi, hi, qi: (bi, qi, hi)),
            pl.BlockSpec((1, s, HEAD_PAD), lambda bi, hi, qi: (bi, 0, hi)),
            pl.BlockSpec((1, s, V_HEAD), lambda bi, hi, qi: (bi, 0, hi)),
        ],
        out_specs=pl.BlockSpec((1, tq, V_HEAD), lambda bi, hi, qi: (bi, qi, hi)),
        out_shape=jax.ShapeDtypeStruct((b, s, MLA_HEADS * V_HEAD), BF16),
        compiler_params=_cparams(("arbitrary", "arbitrary", "arbitrary")),
        name="attention",
    )(q, k, v)


def _attn_out_kernel(x_ref, o_ref, w_ref, moda_ref, modb_ref, g_ref, x1_ref, h_ref, *, d):
    y = _dot(o_ref[0], w_ref[...])
    x1 = x_ref[0] + moda_ref[0][:, 2 * d:3 * d] * y
    x1_ref[0] = x1
    h_ref[0] = _adaln(x1, g_ref[...], modb_ref[0], d).astype(BF16)


def _attn_out_call(x, o, w, moda, modb, g, tm):
    b, s, d = x.shape
    tile = lambda: pl.BlockSpec((1, tm, d), lambda bi, si: (bi, si, 0))
    modspec = lambda: pl.BlockSpec((1, 1, 3 * d), lambda bi, si: (bi, 0, 0))
    return pl.pallas_call(
        functools.partial(_attn_out_kernel, d=d),
        grid=(b, s // tm),
        in_specs=[tile(), tile(), pl.BlockSpec(w.shape, lambda bi, si: (0, 0)),
                  modspec(), modspec(), pl.BlockSpec(g.shape, lambda bi, si: (0, 0))],
        out_specs=[tile(), tile()],
        out_shape=[jax.ShapeDtypeStruct((b, s, d), F32), jax.ShapeDtypeStruct((b, s, d), BF16)],
        compiler_params=_cparams(("arbitrary", "arbitrary")),
        name="attn_out",
    )(x, o, w, moda, modb, g)


def _ffn_body(valid, h_bf, wg_ref, wu_ref, wd_ref, o_ref):
    @pl.when(valid > 0)
    def _():
        h = h_bf[...]
        g = _dot(h, wg_ref[0])
        u = _dot(h, wu_ref[0])
        a = (g * jax.nn.sigmoid(g) * u).astype(BF16)
        o_ref[...] += _dot(a, wd_ref[0])


def _ffn_dense_kernel(te_ref, tv_ref, h_ref, wg_ref, wu_ref, wd_ref, o_ref):
    i = pl.program_id(0)
    j = pl.program_id(1)

    @pl.when(j == 0)
    def _():
        o_ref[...] = jnp.zeros_like(o_ref)

    _ffn_body(tv_ref[i], h_ref, wg_ref, wu_ref, wd_ref, o_ref)


def _ffn_gather_kernel(te_ref, tv_ref, src_ref, h_hbm, wg_ref, wu_ref, wd_ref, o_ref,
                       hf_ref, hb_ref, sem, *, tm):
    i = pl.program_id(0)
    j = pl.program_id(1)
    valid = tv_ref[i]

    def row_copy(r):
        return pltpu.make_async_copy(h_hbm.at[pl.ds(src_ref[0, 0, r], 1), :],
                                     hf_ref.at[pl.ds(r, 1), :], sem)

    @pl.when(j == 0)
    def _():
        o_ref[...] = jnp.zeros_like(o_ref)

    @pl.when(jnp.logical_and(j == 0, valid > 0))
    def _():
        def issue(r, c):
            row_copy(r).start()
            return c

        def wait(r, c):
            row_copy(r).wait()
            return c

        lax.fori_loop(0, tm, issue, 0)
        lax.fori_loop(0, tm, wait, 0)
        hb_ref[...] = hf_ref[...].astype(BF16)

    _ffn_body(valid, hb_ref, wg_ref, wu_ref, wd_ref, o_ref)


def _ffn_call(h, wg, wu, wd, tile_expert, tile_valid, src, tm, tf):
    d = wg.shape[1]
    f = wg.shape[2]
    nf = f // tf
    nt = tile_expert.shape[0]

    def jj(i, j, tv):
        return jnp.where(tv[i] > 0, j, nf - 1)

    w_specs = [
        pl.BlockSpec((1, d, tf), lambda i, j, te, tv: (te[i], 0, jj(i, j, tv))),
        pl.BlockSpec((1, d, tf), lambda i, j, te, tv: (te[i], 0, jj(i, j, tv))),
        pl.BlockSpec((1, tf, d), lambda i, j, te, tv: (te[i], jj(i, j, tv), 0)),
    ]
    out_spec = pl.BlockSpec((tm, d), lambda i, j, te, tv: (i, 0))
    out_shape = jax.ShapeDtypeStruct((nt * tm, d), F32)
    if src is None:
        grid_spec = pltpu.PrefetchScalarGridSpec(
            num_scalar_prefetch=2, grid=(nt, nf),
            in_specs=[pl.BlockSpec((tm, d), lambda i, j, te, tv: (i, 0))] + w_specs,
            out_specs=out_spec)
        return pl.pallas_call(
            _ffn_dense_kernel, grid_spec=grid_spec, out_shape=out_shape,
            compiler_params=_cparams(("arbitrary", "arbitrary")), name="ffn_dense",
        )(tile_expert, tile_valid, h, wg, wu, wd)
    grid_spec = pltpu.PrefetchScalarGridSpec(
        num_scalar_prefetch=2, grid=(nt, nf),
        in_specs=[pl.BlockSpec((1, 1, tm), lambda i, j, te, tv: (i, 0, 0),
                               memory_space=pltpu.SMEM),
                  pl.BlockSpec(memory_space=pl.ANY)] + w_specs,
        out_specs=out_spec,
        scratch_shapes=[pltpu.VMEM((tm, d), F32), pltpu.VMEM((tm, d), BF16),
                        pltpu.SemaphoreType.DMA(())])
    return pl.pallas_call(
        functools.partial(_ffn_gather_kernel, tm=tm), grid_spec=grid_spec, out_shape=out_shape,
        compiler_params=_cparams(("arbitrary", "arbitrary")), name="ffn_moe",
    )(tile_expert, tile_valid, src, h, wg, wu, wd)


def _conv_front_kernel(x1_ref, y_ref, moda_ref, modb_ref, g_ref, w_ref, b_ref,
                       x2_ref, glu_ref, *, d):
    x2 = x1_ref[0] + moda_ref[0][:, 2 * d:3 * d] * y_ref[0]
    x2_ref[0] = x2
    h = _adaln(x2, g_ref[...], modb_ref[0], d).astype(BF16)
    u = _dot(h, w_ref[...]) + b_ref[...]
    glu_ref[0] = u[:, 0:d] * jax.nn.sigmoid(u[:, d:2 * d])


def _conv_front_call(x1, y, moda, modb, g, w, bias, tm):
    b, s, d = x1.shape
    tile = lambda: pl.BlockSpec((1, tm, d), lambda bi, si: (bi, si, 0))
    modspec = lambda: pl.BlockSpec((1, 1, 3 * d), lambda bi, si: (bi, 0, 0))
    const = lambda a: pl.BlockSpec(a.shape, lambda bi, si: (0,) * a.ndim)
    return pl.pallas_call(
        functools.partial(_conv_front_kernel, d=d),
        grid=(b, s // tm),
        in_specs=[tile(), tile(), modspec(), modspec(), const(g), const(w), const(bias)],
        out_specs=[tile(), tile()],
        out_shape=[jax.ShapeDtypeStruct((b, s, d), F32), jax.ShapeDtypeStruct((b, s, d), F32)],
        compiler_params=_cparams(("arbitrary", "arbitrary")),
        name="conv_front",
    )(x1, y, moda, modb, g, w, bias)


def _conv_back_kernel(glu_ref, halo_ref, x2_ref, moda_ref, modb_ref, wdw_ref, bdw_ref,
                      lng_ref, lnb_ref, w2_ref, b2_ref, g_ref, wrh_ref, wrl_ref,
                      x3_ref, h4_ref, route_ref, ext_ref, conv_ref, base_ref, *, d, tm, rows, cols):
    bi = pl.program_id(0)
    si = pl.program_id(1)

    @pl.when(jnp.logical_and(bi == 0, si == 0))
    def _():
        base_ref[...] = jnp.zeros_like(base_ref)

    ext_ref[0:HALO, :] = jnp.where(si == 0, 0.0, halo_ref[0])
    ext_ref[HALO:HALO + tm, :] = glu_ref[0]

    off = HALO - (CONV_WIDTH - 1)
    for r0 in range(0, tm, rows):
        for c0 in range(0, d, cols):
            acc = jnp.zeros((rows, cols), F32) + bdw_ref[:, c0:c0 + cols]
            for k in range(CONV_WIDTH):
                acc = acc + wdw_ref[k:k + 1, c0:c0 + cols] * ext_ref[r0 + off + k:r0 + off + k + rows,
                                                                   c0:c0 + cols]
            conv_ref[r0:r0 + rows, c0:c0 + cols] = acc

    u = conv_ref[...]
    mu = jnp.mean(u, axis=-1, keepdims=True)
    uc = u - mu
    var = jnp.mean(uc * uc, axis=-1, keepdims=True)
    u = uc * lax.rsqrt(var + EPS) * lng_ref[...] + lnb_ref[...]
    u = (u * jax.nn.sigmoid(u)).astype(BF16)
    y = _dot(u, w2_ref[...]) + b2_ref[...]
    x3 = x2_ref[0] + moda_ref[0][:, 2 * d:3 * d] * y
    x3_ref[0] = x3
    h4 = _adaln(x3, g_ref[...], modb_ref[0], d)
    h4_ref[0] = h4

    hh = h4.astype(BF16)
    hl = (h4 - hh.astype(F32)).astype(BF16)
    logits = _dot(hh, wrh_ref[...]) + _dot(hl, wrh_ref[...]) + _dot(hh, wrl_ref[...])
    lane = lax.broadcasted_iota(jnp.int32, (tm, LANES), 1).astype(F32)
    lg = jnp.where(lane < N_EXPERTS, logits, -jnp.inf)
    m1 = jnp.max(lg, axis=-1, keepdims=True)
    i1 = jnp.min(jnp.where(lg == m1, lane, float(LANES)), axis=-1, keepdims=True)
    lg2 = jnp.where(lane == i1, -jnp.inf, lg)
    m2 = jnp.max(lg2, axis=-1, keepdims=True)
    i2 = jnp.min(jnp.where(lg2 == m2, lane, float(LANES)), axis=-1, keepdims=True)
    e2 = jnp.exp(m2 - m1)
    w0 = 1.0 / (1.0 + e2)
    w1 = e2 / (1.0 + e2)
    oh0 = (lane == i1).astype(F32)
    oh1 = (lane == i2).astype(F32)
    both = oh0 + oh1
    tri = (lax.broadcasted_iota(jnp.int32, (tm, tm), 0)
           > lax.broadcasted_iota(jnp.int32, (tm, tm), 1)).astype(BF16)
    cnt = _dot(tri, both.astype(BF16)) + base_ref[...]
    r0 = jnp.sum(cnt * oh0, axis=-1, keepdims=True)
    r1 = jnp.sum(cnt * oh1, axis=-1, keepdims=True)
    base_ref[...] = base_ref[...] + jnp.sum(both, axis=0, keepdims=True)
    route = jnp.where(lane == 0.0, i1, 0.0)
    route = jnp.where(lane == 1.0, i2, route)
    route = jnp.where(lane == 2.0, w0, route)
    route = jnp.where(lane == 3.0, w1, route)
    route = jnp.where(lane == 4.0, r0, route)
    route = jnp.where(lane == 5.0, r1, route)
    route_ref[0] = route


def _conv_back_call(glu, x2, moda, modb, wdw, bdw, lng, lnb, w2, b2, g, wrh, wrl, tm):
    b, s, d = x2.shape
    tile = lambda: pl.BlockSpec((1, tm, d), lambda bi, si: (bi, si, 0))
    modspec = lambda: pl.BlockSpec((1, 1, 3 * d), lambda bi, si: (bi, 0, 0))
    const = lambda a: pl.BlockSpec(a.shape, lambda bi, si: (0,) * a.ndim)
    hpt = tm // HALO
    halo = pl.BlockSpec((1, HALO, d), lambda bi, si: (bi, jnp.maximum(si * hpt - 1, 0), 0))
    return pl.pallas_call(
        functools.partial(_conv_back_kernel, d=d, tm=tm, rows=64, cols=256),
        grid=(b, s // tm),
        in_specs=[tile(), halo, tile(), modspec(), modspec(), const(wdw), const(bdw),
                  const(lng), const(lnb), const(w2), const(b2), const(g), const(wrh), const(wrl)],
        out_specs=[tile(), tile(), pl.BlockSpec((1, tm, LANES), lambda bi, si: (bi, si, 0))],
        out_shape=[jax.ShapeDtypeStruct((b, s, d), F32), jax.ShapeDtypeStruct((b, s, d), F32),
                   jax.ShapeDtypeStruct((b, s, LANES), F32)],
        scratch_shapes=[pltpu.VMEM((tm + HALO, d), F32), pltpu.VMEM((tm, d), F32),
                        pltpu.VMEM((1, LANES), F32)],
        compiler_params=_cparams(("arbitrary", "arbitrary")),
        name="conv_back",
    )(glu, glu, x2, moda, modb, wdw, bdw, lng, lnb, w2, b2, g, wrh, wrl)


def _combine_kernel(p0_ref, p1_ref, x_ref, route_ref, mod_ref, fg_ref, ys_hbm, o_ref,
                    y0_ref, y1_ref, sem, *, d, tm):
    def copies(r):
        return (pltpu.make_async_copy(ys_hbm.at[pl.ds(p0_ref[0, 0, r], 1), :],
                                      y0_ref.at[pl.ds(r, 1), :], sem.at[0]),
                pltpu.make_async_copy(ys_hbm.at[pl.ds(p1_ref[0, 0, r], 1), :],
                                      y1_ref.at[pl.ds(r, 1), :], sem.at[1]))

    def issue(r, c):
        a, b = copies(r)
        a.start()
        b.start()
        return c

    def wait(r, c):
        a, b = copies(r)
        a.wait()
        b.wait()
        return c

    lax.fori_loop(0, tm, issue, 0)
    lax.fori_loop(0, tm, wait, 0)
    w0 = route_ref[:, 2:3]
    w1 = route_ref[:, 3:4]
    y = w0 * y0_ref[...] + w1 * y1_ref[...]
    x = x_ref[...] + mod_ref[0][:, 2 * d:3 * d] * y
    o_ref[...] = _rms(x, fg_ref[...])


def _combine_call(p0, p1, x3, route, mod, fg, ys, tm, tiles_per_batch):
    t, d = x3.shape
    return pl.pallas_call(
        functools.partial(_combine_kernel, d=d, tm=tm),
        grid=(t // tm,),
        in_specs=[
            pl.BlockSpec((1, 1, tm), lambda i: (i, 0, 0), memory_space=pltpu.SMEM),
            pl.BlockSpec((1, 1, tm), lambda i: (i, 0, 0), memory_space=pltpu.SMEM),
            pl.BlockSpec((tm, d), lambda i: (i, 0)),
            pl.BlockSpec((tm, LANES), lambda i: (i, 0)),
            pl.BlockSpec((1, 1, 3 * d), lambda i: (i // tiles_per_batch, 0, 0)),
            pl.BlockSpec(fg.shape, lambda i: (0, 0)),
            pl.BlockSpec(memory_space=pl.ANY),
        ],
        out_specs=pl.BlockSpec((tm, d), lambda i: (i, 0)),
        out_shape=jax.ShapeDtypeStruct((t, d), F32),
        scratch_shapes=[pltpu.VMEM((tm, d), F32), pltpu.VMEM((tm, d), F32),
                        pltpu.SemaphoreType.DMA((2,))],
        compiler_params=_cparams(("arbitrary",)),
        name="combine",
    )(p0, p1, x3, route, mod, fg, ys)


def _rope_consts():
    lane = np.arange(LANES)
    half = QK_ROPE // 2
    freq = np.where(lane < QK_ROPE, ROPE_THETA ** (-(2.0 * (lane % half)) / QK_ROPE), 0.0)
    sa = np.where((lane >= half) & (lane < QK_ROPE), 1.0, 0.0)
    sb = np.where(lane < half, -1.0, 0.0)
    rc = np.zeros((8, LANES), np.float32)
    rc[0], rc[1], rc[2] = freq, sa, sb
    return jnp.asarray(rc)


def _moe_plan(route, tm, n_tiles):
    t = route.shape[0]
    e0 = route[:, 0].astype(jnp.int32)
    e1 = route[:, 1].astype(jnp.int32)
    r0 = route[:, 4].astype(jnp.int32)
    r1 = route[:, 5].astype(jnp.int32)
    ids = jnp.arange(N_EXPERTS, dtype=jnp.int32)
    counts = (jnp.sum(e0[:, None] == ids[None, :], axis=0)
              + jnp.sum(e1[:, None] == ids[None, :], axis=0)).astype(jnp.int32)
    tiles = (counts + tm - 1) // tm
    tile_end = jnp.cumsum(tiles)
    tile_start = tile_end - tiles
    pos0 = tile_start[e0] * tm + r0
    pos1 = tile_start[e1] * tm + r1
    tok = jnp.arange(t, dtype=jnp.int32)
    src = jnp.zeros((n_tiles * tm,), jnp.int32).at[pos0].set(tok).at[pos1].set(tok)
    ti = jnp.arange(n_tiles, dtype=jnp.int32)
    used = tile_end[-1]
    te = jnp.searchsorted(tile_end, jnp.minimum(ti, used - 1), side="right").astype(jnp.int32)
    te = jnp.minimum(te, N_EXPERTS - 1)
    tv = jnp.clip(counts[te] - (ti - tile_start[te]) * tm, 0, tm)
    tv = jnp.where(ti < used, tv, 0).astype(jnp.int32)
    return pos0, pos1, src.reshape(n_tiles, 1, tm), te, tv


def kernel(x, c, positions, ada_w, ada_b, norm_g, mla_w_in, mla_g_q, mla_g_kv, mla_w_uq,
           mla_w_ukv, mla_w_out, conv_w_pw1, conv_b_pw1, conv_w_dw, conv_b_dw, conv_ln_g,
           conv_ln_b, conv_w_pw2, conv_b_pw2, ffn_w_gate, ffn_w_up, ffn_w_down, moe_w_router,
           moe_w_gate, moe_w_up, moe_w_down, final_g):
    b, s, d = x.shape
    t = b * s
    f = ffn_w_gate.shape[-1]
    assert ada_w.shape[0] == 2 and mla_w_in.shape[0] == 1 and conv_w_pw1.shape[0] == 1
    assert moe_w_gate.shape[1] == N_EXPERTS and conv_w_dw.shape[1] == CONV_WIDTH

    scale = (QK_NOPE + QK_ROPE) ** -0.5
    w_in = jnp.pad(mla_w_in[0], ((0, 0), (0, LANES - QK_ROPE))).astype(BF16)
    wq = mla_w_uq[0].reshape(Q_LORA, MLA_HEADS, QK_NOPE + QK_ROPE) * scale
    wq = jnp.pad(wq, ((0, 0), (0, 0), (0, HEAD_PAD - QK_NOPE - QK_ROPE)))
    wq = wq.reshape(Q_LORA, MLA_HEADS * HEAD_PAD).astype(BF16)
    wkv = mla_w_ukv[0].reshape(KV_LORA, MLA_HEADS, QK_NOPE + V_HEAD)
    wk = wkv[:, :, :QK_NOPE].reshape(KV_LORA, MLA_HEADS * QK_NOPE).astype(BF16)
    wv = wkv[:, :, QK_NOPE:].reshape(KV_LORA, MLA_HEADS * V_HEAD).astype(BF16)
    w_out = mla_w_out[0].astype(BF16)
    w_pw1 = conv_w_pw1[0].astype(BF16)
    w_pw2 = conv_w_pw2[0].astype(BF16)
    wr = jnp.pad(moe_w_router[0], ((0, 0), (0, LANES - N_EXPERTS)))
    wr_hi = wr.astype(BF16)
    wr_lo = (wr - wr_hi.astype(F32)).astype(BF16)
    row = lambda v: v.reshape(1, -1)

    mods = _ada_call(c, ada_w.reshape(4, d, 3 * d), ada_b.reshape(4, 3 * d))
    mod = [mods[i].reshape(b, 1, 3 * d) for i in range(4)]

    q, k, v = _mla_front_call(x, positions.reshape(b, s, 1), mod[0], row(norm_g[0, 0]), w_in,
                              row(mla_g_q[0]), row(mla_g_kv[0]), wq, wk, wv, _rope_consts(),
                              tm=256)
    o = _attn_call(q, k, v, tq=512, tk=512)
    x1, h2 = _attn_out_call(x, o, w_out, mod[0], mod[1], row(norm_g[0, 1]), tm=512)
    tm_d = 1024
    nt_d = t // tm_d
    y1 = _ffn_call(h2.reshape(t, d), ffn_w_gate.astype(BF16), ffn_w_up.astype(BF16),
                   ffn_w_down.astype(BF16), jnp.zeros((nt_d,), jnp.int32),
                   jnp.full((nt_d,), tm_d, jnp.int32), None, tm=tm_d, tf=512)

    x2, glu = _conv_front_call(x1, y1.reshape(b, s, d), mod[1], mod[2], row(norm_g[1, 0]),
                               w_pw1, row(conv_b_pw1[0]), tm=256)
    x3, h4, route = _conv_back_call(glu, x2, mod[2], mod[3], conv_w_dw[0], row(conv_b_dw[0]),
                                    row(conv_ln_g[0]), row(conv_ln_b[0]), w_pw2,
                                    row(conv_b_pw2[0]), row(norm_g[1, 1]), wr_hi, wr_lo, tm=256)
    route = route.reshape(t, LANES)
    tm_e = 512
    nt_e = (2 * t) // tm_e + N_EXPERTS
    pos0, pos1, src, te, tv = _moe_plan(route, tm_e, nt_e)
    ys = _ffn_call(h4.reshape(t, d), moe_w_gate[0].astype(BF16), moe_w_up[0].astype(BF16),
                   moe_w_down[0].astype(BF16), te, tv, src, tm=tm_e, tf=512)
    tm_c = 256
    out = _combine_call(pos0.reshape(t // tm_c, 1, tm_c), pos1.reshape(t // tm_c, 1, tm_c),
                        x3.reshape(t, d), route, mod[3], row(final_g), ys, tm=tm_c,
                        tiles_per_batch=s // tm_c)
    return out.reshape(b, s, d)
```

```python
import functools

import numpy as np
import jax
import jax.numpy as jnp
from jax import lax
from jax.experimental import pallas as pl
from jax.experimental.pallas import tpu as pltpu

F32 = jnp.float32
BF16 = jnp.bfloat16

MLA_HEADS = 16
Q_LORA = 512
KV_LORA = 512
QK_NOPE = 128
QK_ROPE = 64
V_HEAD = 128
ROPE_THETA = 10000.0
CONV_WIDTH = 31
N_EXPERTS = 8
EPS = 1e-6

LANES = 128
SUBLANES = 8
HEAD_PAD = 2 * LANES
HALO = 32
VMEM_LIMIT = 56 * 1024 * 1024


def _cparams(sem):
    return pltpu.CompilerParams(dimension_semantics=sem, vmem_limit_bytes=VMEM_LIMIT)


def _rms(x, g):
    return x * lax.rsqrt(jnp.mean(x * x, axis=-1, keepdims=True) + EPS) * g


def _adaln(x, g, mod, d):
    return _rms(x, g) * (1.0 + mod[:, d:2 * d]) + mod[:, 0:d]


def _dot(a, b):
    return jnp.dot(a, b, preferred_element_type=F32)


def _ada_kernel(c_ref, w_ref, b_ref, o_ref):
    c = c_ref[...]
    sc = (c * jax.nn.sigmoid(c)).astype(BF16)
    o_ref[0] = _dot(sc, w_ref[0].astype(BF16)) + b_ref[0]


def _ada_call(c, ada_w, ada_b):
    n, d, d3 = ada_w.shape
    b = c.shape[0]
    tn = 1024
    return pl.pallas_call(
        _ada_kernel,
        grid=(n, d3 // tn),
        in_specs=[
            pl.BlockSpec((b, d), lambda i, j: (0, 0)),
            pl.BlockSpec((1, d, tn), lambda i, j: (i, 0, j)),
            pl.BlockSpec((1, 1, tn), lambda i, j: (i, 0, j)),
        ],
        out_specs=pl.BlockSpec((1, b, tn), lambda i, j: (i, 0, j)),
        out_shape=jax.ShapeDtypeStruct((n, b, d3), F32),
        compiler_params=_cparams(("arbitrary", "arbitrary")),
        name="ada",
    )(c, ada_w, ada_b.reshape(n, 1, d3))


def _rope_tile(t, cosv, sa, sb):
    return t * cosv + pltpu.roll(t, 32, 1) * sa + pltpu.roll(t, 96, 1) * sb


def _mla_front_kernel(x_ref, pos_ref, mod_ref, g_ref, win_ref, gq_ref, gkv_ref,
                      wq_ref, wk_ref, wv_ref, rc_ref, q_ref, k_ref, v_ref, *, d):
    x = x_ref[0]
    h = _adaln(x, g_ref[...], mod_ref[0], d).astype(BF16)
    lat = _dot(h, win_ref[...])
    cq = _rms(lat[:, 0:Q_LORA], gq_ref[...]).astype(BF16)
    ckv = _rms(lat[:, Q_LORA:Q_LORA + KV_LORA], gkv_ref[...]).astype(BF16)
    kr = lat[:, Q_LORA + KV_LORA:Q_LORA + KV_LORA + LANES]

    ang = pos_ref[0].astype(F32) * rc_ref[0:1, :]
    cosv = jnp.cos(ang)
    sinv = jnp.sin(ang)
    sa = sinv * rc_ref[1:2, :]
    sb = sinv * rc_ref[2:3, :]

    q = _dot(cq, wq_ref[...])
    for hh in range(MLA_HEADS):
        lo = hh * HEAD_PAD
        q_ref[0, :, lo:lo + LANES] = q[:, lo:lo + LANES].astype(BF16)
        q_ref[0, :, lo + LANES:lo + HEAD_PAD] = _rope_tile(
            q[:, lo + LANES:lo + HEAD_PAD], cosv, sa, sb).astype(BF16)

    kn = _dot(ckv, wk_ref[...])
    krr = _rope_tile(kr, cosv, sa, sb).astype(BF16)
    for hh in range(MLA_HEADS):
        lo = hh * HEAD_PAD
        k_ref[0, :, lo:lo + LANES] = kn[:, hh * LANES:(hh + 1) * LANES].astype(BF16)
        k_ref[0, :, lo + LANES:lo + HEAD_PAD] = krr
    v_ref[0] = _dot(ckv, wv_ref[...]).astype(BF16)


def _mla_front_call(x, pos3, mod, g, win, gq, gkv, wq, wk, wv, rc, tm):
    b, s, d = x.shape
    hq = MLA_HEADS * HEAD_PAD
    hv = MLA_HEADS * V_HEAD
    const = lambda shape: pl.BlockSpec(shape, lambda bi, si: (0,) * len(shape))
    return pl.pallas_call(
        functools.partial(_mla_front_kernel, d=d),
        grid=(b, s // tm),
        in_specs=[
            pl.BlockSpec((1, tm, d), lambda bi, si: (bi, si, 0)),
            pl.BlockSpec((1, tm, 1), lambda bi, si: (bi, si, 0)),
            pl.BlockSpec((1, 1, 3 * d), lambda bi, si: (bi, 0, 0)),
            const(g.shape), const(win.shape), const(gq.shape), const(gkv.shape),
            const(wq.shape), const(wk.shape), const(wv.shape), const(rc.shape),
        ],
        out_specs=[
            pl.BlockSpec((1, tm, hq), lambda bi, si: (bi, si, 0)),
            pl.BlockSpec((1, tm, hq), lambda bi, si: (bi, si, 0)),
            pl.BlockSpec((1, tm, hv), lambda bi, si: (bi, si, 0)),
        ],
        out_shape=[
            jax.ShapeDtypeStruct((b, s, hq), BF16),
            jax.ShapeDtypeStruct((b, s, hq), BF16),
            jax.ShapeDtypeStruct((b, s, hv), BF16),
        ],
        compiler_params=_cparams(("arbitrary", "arbitrary")),
        name="mla_front",
    )(x, pos3, mod, g, win, gq, gkv, wq, wk, wv, rc)


def _attn_kernel(q_ref, k_ref, v_ref, o_ref, *, t, heads):
    qi = pl.program_id(2)
    tril = (lax.broadcasted_iota(jnp.int32, (t, t), 1)
            <= lax.broadcasted_iota(jnp.int32, (t, t), 0))

    def block(j, carry, masked):
        start = pl.multiple_of(j * t, t)
        out = []
        for hh in range(heads):
            m, l, acc = carry[hh]
            q = q_ref[0, :, hh * HEAD_PAD:(hh + 1) * HEAD_PAD]
            k = k_ref[0, pl.ds(start, t), hh * HEAD_PAD:(hh + 1) * HEAD_PAD]
            v = v_ref[0, pl.ds(start, t), hh * V_HEAD:(hh + 1) * V_HEAD]
            s = lax.dot_general(q, k, (((1,), (1,)), ((), ())), preferred_element_type=F32)
            if masked:
                s = jnp.where(tril, s, -jnp.inf)
            m_new = jnp.maximum(m, jnp.max(s, axis=-1, keepdims=True))
            alpha = jnp.exp(m - m_new)
            p = jnp.exp(s - m_new)
            l = alpha * l + jnp.sum(p, axis=-1, keepdims=True)
            acc = alpha * acc + _dot(p.astype(BF16), v)
            out.append((m_new, l, acc))
        return tuple(out)

    init = tuple((jnp.full((t, 1), -jnp.inf, F32), jnp.zeros((t, 1), F32),
                  jnp.zeros((t, V_HEAD), F32)) for _ in range(heads))
    carry = lax.fori_loop(0, qi, functools.partial(block, masked=False), init)
    carry = block(qi, carry, True)
    for hh in range(heads):
        _, l, acc = carry[hh]
        o_ref[0, :, hh * V_HEAD:(hh + 1) * V_HEAD] = (acc / l).astype(BF16)


def _attn_call(q, k, v, t, heads):
    b, s, _ = q.shape
    return pl.pallas_call(
        functools.partial(_attn_kernel, t=t, heads=heads),
        grid=(b, MLA_HEADS // heads, s // t),
        in_specs=[
            pl.BlockSpec((1, t, heads * HEAD_PAD), lambda bi, hi, qi: (bi, qi, hi)),
            pl.BlockSpec((1, s, heads * HEAD_PAD), lambda bi, hi, qi: (bi, 0, hi)),
            pl.BlockSpec((1, s, heads * V_HEAD), lambda bi, hi, qi: (bi, 0, hi)),
        ],
        out_specs=pl.BlockSpec((1, t, heads * V_HEAD), lambda bi, hi, qi: (bi, qi, hi)),
        out_shape=jax.ShapeDtypeStruct((b, s, MLA_HEADS * V_HEAD), BF16),
        compiler_params=_cparams(("arbitrary", "arbitrary", "arbitrary")),
        name="attention",
    )(q, k, v)


def _attn_out_kernel(x_ref, o_ref, w_ref, moda_ref, modb_ref, g_ref, x1_ref, h_ref, *, d):
    y = _dot(o_ref[0], w_ref[...])
    x1 = x_ref[0] + moda_ref[0][:, 2 * d:3 * d] * y
    x1_ref[0] = x1
    h_ref[0] = _adaln(x1, g_ref[...], modb_ref[0], d).astype(BF16)


def _attn_out_call(x, o, w, moda, modb, g, tm):
    b, s, d = x.shape
    tile = lambda: pl.BlockSpec((1, tm, d), lambda bi, si: (bi, si, 0))
    modspec = lambda: pl.BlockSpec((1, 1, 3 * d), lambda bi, si: (bi, 0, 0))
    return pl.pallas_call(
        functools.partial(_attn_out_kernel, d=d),
        grid=(b, s // tm),
        in_specs=[tile(), tile(), pl.BlockSpec(w.shape, lambda bi, si: (0, 0)),
                  modspec(), modspec(), pl.BlockSpec(g.shape, lambda bi, si: (0, 0))],
        out_specs=[tile(), tile()],
        out_shape=[jax.ShapeDtypeStruct((b, s, d), F32), jax.ShapeDtypeStruct((b, s, d), BF16)],
        compiler_params=_cparams(("arbitrary", "arbitrary")),
        name="attn_out",
    )(x, o, w, moda, modb, g)


def _ffn_body(valid, h_bf, wg_ref, wu_ref, wd_ref, o_ref, tm, sub):
    for sb in range(tm // sub):
        @pl.when(sb * sub < valid)
        def _():
            rows = slice(sb * sub, (sb + 1) * sub)
            h = h_bf[rows, :]
            g = _dot(h, wg_ref[0].astype(BF16))
            u = _dot(h, wu_ref[0].astype(BF16))
            a = (g * jax.nn.sigmoid(g) * u).astype(BF16)
            o_ref[rows, :] += _dot(a, wd_ref[0].astype(BF16))


def _ffn_dense_kernel(te_ref, tv_ref, h_ref, wg_ref, wu_ref, wd_ref, o_ref, *, tm, sub):
    i = pl.program_id(0)
    j = pl.program_id(1)

    @pl.when(j == 0)
    def _():
        o_ref[...] = jnp.zeros_like(o_ref)

    _ffn_body(tv_ref[i], h_ref, wg_ref, wu_ref, wd_ref, o_ref, tm, sub)


def _ffn_gather_kernel(te_ref, tv_ref, src_ref, h_hbm, wg_ref, wu_ref, wd_ref, o_ref,
                       hf_ref, hb_ref, sem, *, tm, sub):
    i = pl.program_id(0)
    j = pl.program_id(1)
    valid = tv_ref[i]

    @pl.when(j == 0)
    def _():
        o_ref[...] = jnp.zeros_like(o_ref)
        for sb in range(tm // sub):
            @pl.when(sb * sub < valid)
            def _():
                def row_copy(r):
                    return pltpu.make_async_copy(
                        h_hbm.at[pl.ds(src_ref[0, 0, sb * sub + r], 1), :],
                        hf_ref.at[pl.ds(r, 1), :], sem)

                def issue(r, c):
                    row_copy(r).start()
                    return c

                def wait(r, c):
                    row_copy(r).wait()
                    return c

                lax.fori_loop(0, sub, issue, 0)
                lax.fori_loop(0, sub, wait, 0)
                hb_ref[sb * sub:(sb + 1) * sub, :] = hf_ref[...].astype(BF16)

    _ffn_body(valid, hb_ref, wg_ref, wu_ref, wd_ref, o_ref, tm, sub)


def _ffn_call(h, wg, wu, wd, tile_expert, tile_valid, src, tm, sub, tf):
    d = wg.shape[1]
    f = wg.shape[2]
    nf = f // tf
    nt = tile_expert.shape[0]

    def jj(i, j, tv):
        return jnp.where(tv[i] > 0, j, nf - 1)

    w_specs = [
        pl.BlockSpec((1, d, tf), lambda i, j, te, tv: (te[i], 0, jj(i, j, tv))),
        pl.BlockSpec((1, d, tf), lambda i, j, te, tv: (te[i], 0, jj(i, j, tv))),
        pl.BlockSpec((1, tf, d), lambda i, j, te, tv: (te[i], jj(i, j, tv), 0)),
    ]
    out_spec = pl.BlockSpec((tm, d), lambda i, j, te, tv: (i, 0))
    out_shape = jax.ShapeDtypeStruct((nt * tm, d), F32)
    if src is None:
        grid_spec = pltpu.PrefetchScalarGridSpec(
            num_scalar_prefetch=2, grid=(nt, nf),
            in_specs=[pl.BlockSpec((tm, d), lambda i, j, te, tv: (i, 0))] + w_specs,
            out_specs=out_spec)
        return pl.pallas_call(
            functools.partial(_ffn_dense_kernel, tm=tm, sub=sub), grid_spec=grid_spec,
            out_shape=out_shape,
            compiler_params=_cparams(("arbitrary", "arbitrary")), name="ffn_dense",
        )(tile_expert, tile_valid, h, wg, wu, wd)
    grid_spec = pltpu.PrefetchScalarGridSpec(
        num_scalar_prefetch=2, grid=(nt, nf),
        in_specs=[pl.BlockSpec((1, 1, tm), lambda i, j, te, tv: (i, 0, 0),
                               memory_space=pltpu.SMEM),
                  pl.BlockSpec(memory_space=pl.ANY)] + w_specs,
        out_specs=out_spec,
        scratch_shapes=[pltpu.VMEM((sub, d), F32), pltpu.VMEM((tm, d), BF16),
                        pltpu.SemaphoreType.DMA(())])
    return pl.pallas_call(
        functools.partial(_ffn_gather_kernel, tm=tm, sub=sub), grid_spec=grid_spec,
        out_shape=out_shape,
        compiler_params=_cparams(("arbitrary", "arbitrary")), name="ffn_moe",
    )(tile_expert, tile_valid, src, h, wg, wu, wd)


def _conv_front_kernel(x1_ref, y_ref, moda_ref, modb_ref, g_ref, w_ref, b_ref,
                       x2_ref, glu_ref, *, d):
    x2 = x1_ref[0] + moda_ref[0][:, 2 * d:3 * d] * y_ref[0]
    x2_ref[0] = x2
    h = _adaln(x2, g_ref[...], modb_ref[0], d).astype(BF16)
    u = _dot(h, w_ref[...]) + b_ref[...]
    glu_ref[0] = u[:, 0:d] * jax.nn.sigmoid(u[:, d:2 * d])


def _conv_front_call(x1, y, moda, modb, g, w, bias, tm):
    b, s, d = x1.shape
    tile = lambda: pl.BlockSpec((1, tm, d), lambda bi, si: (bi, si, 0))
    modspec = lambda: pl.BlockSpec((1, 1, 3 * d), lambda bi, si: (bi, 0, 0))
    const = lambda a: pl.BlockSpec(a.shape, lambda bi, si: (0,) * a.ndim)
    return pl.pallas_call(
        functools.partial(_conv_front_kernel, d=d),
        grid=(b, s // tm),
        in_specs=[tile(), tile(), modspec(), modspec(), const(g), const(w), const(bias)],
        out_specs=[tile(), tile()],
        out_shape=[jax.ShapeDtypeStruct((b, s, d), F32), jax.ShapeDtypeStruct((b, s, d), F32)],
        compiler_params=_cparams(("arbitrary", "arbitrary")),
        name="conv_front",
    )(x1, y, moda, modb, g, w, bias)


def _conv_back_kernel(glu_ref, halo_ref, x2_ref, moda_ref, modb_ref, wdw_ref, bdw_ref,
                      lng_ref, lnb_ref, w2_ref, b2_ref, g_ref, wrh_ref, wrl_ref,
                      x3_ref, h4_ref, route_ref, ext_ref, conv_ref, base_ref, sh_ref,
                      *, d, tm, rows, cols):
    bi = pl.program_id(0)
    si = pl.program_id(1)

    @pl.when(jnp.logical_and(bi == 0, si == 0))
    def _():
        base_ref[...] = jnp.zeros_like(base_ref)

    ext_ref[0:HALO, :] = jnp.where(si == 0, 0.0, halo_ref[0])
    ext_ref[HALO:HALO + tm, :] = glu_ref[0]

    off = HALO - (CONV_WIDTH - 1)
    span = sh_ref.shape[1]
    for c0 in range(0, d, cols):
        for r in range(1, SUBLANES):
            sh_ref[r - 1] = ext_ref[r:r + span, c0:c0 + cols]
        for r0 in range(0, tm, rows):
            acc = jnp.zeros((rows, cols), F32) + bdw_ref[:, c0:c0 + cols]
            for k in range(CONV_WIDTH):
                r = (off + k) % SUBLANES
                a = r0 + off + k - r
                if r == 0:
                    tap = ext_ref[a:a + rows, c0:c0 + cols]
                else:
                    tap = sh_ref[r - 1, a:a + rows, :]
                acc = acc + wdw_ref[k:k + 1, c0:c0 + cols] * tap
            conv_ref[r0:r0 + rows, c0:c0 + cols] = acc

    u = conv_ref[...]
    mu = jnp.mean(u, axis=-1, keepdims=True)
    uc = u - mu
    var = jnp.mean(uc * uc, axis=-1, keepdims=True)
    u = uc * lax.rsqrt(var + EPS) * lng_ref[...] + lnb_ref[...]
    u = (u * jax.nn.sigmoid(u)).astype(BF16)
    y = _dot(u, w2_ref[...]) + b2_ref[...]
    x3 = x2_ref[0] + moda_ref[0][:, 2 * d:3 * d] * y
    x3_ref[0] = x3
    h4 = _adaln(x3, g_ref[...], modb_ref[0], d)
    h4_ref[0] = h4

    hh = h4.astype(BF16)
    hl = (h4 - hh.astype(F32)).astype(BF16)
    logits = _dot(hh, wrh_ref[...]) + _dot(hl, wrh_ref[...]) + _dot(hh, wrl_ref[...])
    lane = lax.broadcasted_iota(jnp.int32, (tm, LANES), 1).astype(F32)
    lg = jnp.where(lane < N_EXPERTS, logits, -jnp.inf)
    m1 = jnp.max(lg, axis=-1, keepdims=True)
    i1 = jnp.min(jnp.where(lg == m1, lane, float(LANES)), axis=-1, keepdims=True)
    lg2 = jnp.where(lane == i1, -jnp.inf, lg)
    m2 = jnp.max(lg2, axis=-1, keepdims=True)
    i2 = jnp.min(jnp.where(lg2 == m2, lane, float(LANES)), axis=-1, keepdims=True)
    e2 = jnp.exp(m2 - m1)
    w0 = 1.0 / (1.0 + e2)
    w1 = e2 / (1.0 + e2)
    oh0 = (lane == i1).astype(F32)
    oh1 = (lane == i2).astype(F32)
    both = oh0 + oh1
    tri = (lax.broadcasted_iota(jnp.int32, (tm, tm), 0)
           > lax.broadcasted_iota(jnp.int32, (tm, tm), 1)).astype(BF16)
    cnt = _dot(tri, both.astype(BF16)) + base_ref[...]
    r0 = jnp.sum(cnt * oh0, axis=-1, keepdims=True)
    r1 = jnp.sum(cnt * oh1, axis=-1, keepdims=True)
    base_ref[...] = base_ref[...] + jnp.sum(both, axis=0, keepdims=True)
    route = jnp.where(lane == 0.0, i1, 0.0)
    route = jnp.where(lane == 1.0, i2, route)
    route = jnp.where(lane == 2.0, w0, route)
    route = jnp.where(lane == 3.0, w1, route)
    route = jnp.where(lane == 4.0, r0, route)
    route = jnp.where(lane == 5.0, r1, route)
    route_ref[0] = route


def _conv_back_call(glu, x2, moda, modb, wdw, bdw, lng, lnb, w2, b2, g, wrh, wrl, tm):
    b, s, d = x2.shape
    tile = lambda: pl.BlockSpec((1, tm, d), lambda bi, si: (bi, si, 0))
    modspec = lambda: pl.BlockSpec((1, 1, 3 * d), lambda bi, si: (bi, 0, 0))
    const = lambda a: pl.BlockSpec(a.shape, lambda bi, si: (0,) * a.ndim)
    hpt = tm // HALO
    halo = pl.BlockSpec((1, HALO, d), lambda bi, si: (bi, jnp.maximum(si * hpt - 1, 0), 0))
    cols = 2 * LANES
    return pl.pallas_call(
        functools.partial(_conv_back_kernel, d=d, tm=tm, rows=64, cols=cols),
        grid=(b, s // tm),
        in_specs=[tile(), halo, tile(), modspec(), modspec(), const(wdw), const(bdw),
                  const(lng), const(lnb), const(w2), const(b2), const(g), const(wrh), const(wrl)],
        out_specs=[tile(), tile(), pl.BlockSpec((1, tm, LANES), lambda bi, si: (bi, si, 0))],
        out_shape=[jax.ShapeDtypeStruct((b, s, d), F32), jax.ShapeDtypeStruct((b, s, d), F32),
                   jax.ShapeDtypeStruct((b, s, LANES), F32)],
        scratch_shapes=[pltpu.VMEM((tm + HALO, d), F32), pltpu.VMEM((tm, d), F32),
                        pltpu.VMEM((1, LANES), F32),
                        pltpu.VMEM((SUBLANES - 1, tm + HALO - SUBLANES, cols), F32)],
        compiler_params=_cparams(("arbitrary", "arbitrary")),
        name="conv_back",
    )(glu, glu, x2, moda, modb, wdw, bdw, lng, lnb, w2, b2, g, wrh, wrl)


def _combine_kernel(p0_ref, p1_ref, x_ref, route_ref, mod_ref, fg_ref, ys_hbm, o_ref,
                    y0_ref, y1_ref, sem, *, d, tm):
    def copies(r):
        return (pltpu.make_async_copy(ys_hbm.at[pl.ds(p0_ref[0, 0, r], 1), :],
                                      y0_ref.at[pl.ds(r, 1), :], sem.at[0]),
                pltpu.make_async_copy(ys_hbm.at[pl.ds(p1_ref[0, 0, r], 1), :],
                                      y1_ref.at[pl.ds(r, 1), :], sem.at[1]))

    def issue(r, c):
        a, b = copies(r)
        a.start()
        b.start()
        return c

    def wait(r, c):
        a, b = copies(r)
        a.wait()
        b.wait()
        return c

    lax.fori_loop(0, tm, issue, 0)
    lax.fori_loop(0, tm, wait, 0)
    w0 = route_ref[:, 2:3]
    w1 = route_ref[:, 3:4]
    y = w0 * y0_ref[...] + w1 * y1_ref[...]
    x = x_ref[...] + mod_ref[0][:, 2 * d:3 * d] * y
    o_ref[...] = _rms(x, fg_ref[...])


def _combine_call(p0, p1, x3, route, mod, fg, ys, tm, tiles_per_batch):
    t, d = x3.shape
    return pl.pallas_call(
        functools.partial(_combine_kernel, d=d, tm=tm),
        grid=(t // tm,),
        in_specs=[
            pl.BlockSpec((1, 1, tm), lambda i: (i, 0, 0), memory_space=pltpu.SMEM),
            pl.BlockSpec((1, 1, tm), lambda i: (i, 0, 0), memory_space=pltpu.SMEM),
            pl.BlockSpec((tm, d), lambda i: (i, 0)),
            pl.BlockSpec((tm, LANES), lambda i: (i, 0)),
            pl.BlockSpec((1, 1, 3 * d), lambda i: (i // tiles_per_batch, 0, 0)),
            pl.BlockSpec(fg.shape, lambda i: (0, 0)),
            pl.BlockSpec(memory_space=pl.ANY),
        ],
        out_specs=pl.BlockSpec((tm, d), lambda i: (i, 0)),
        out_shape=jax.ShapeDtypeStruct((t, d), F32),
        scratch_shapes=[pltpu.VMEM((tm, d), F32), pltpu.VMEM((tm, d), F32),
                        pltpu.SemaphoreType.DMA((2,))],
        compiler_params=_cparams(("arbitrary",)),
        name="combine",
    )(p0, p1, x3, route, mod, fg, ys)


def _rope_consts():
    lane = np.arange(LANES)
    half = QK_ROPE // 2
    freq = np.where(lane < QK_ROPE, ROPE_THETA ** (-(2.0 * (lane % half)) / QK_ROPE), 0.0)
    sa = np.where((lane >= half) & (lane < QK_ROPE), 1.0, 0.0)
    sb = np.where(lane < half, -1.0, 0.0)
    rc = np.zeros((8, LANES), np.float32)
    rc[0], rc[1], rc[2] = freq, sa, sb
    return jnp.asarray(rc)


def _moe_plan(route, tm, n_tiles):
    t = route.shape[0]
    e0 = route[:, 0].astype(jnp.int32)
    e1 = route[:, 1].astype(jnp.int32)
    r0 = route[:, 4].astype(jnp.int32)
    r1 = route[:, 5].astype(jnp.int32)
    ids = jnp.arange(N_EXPERTS, dtype=jnp.int32)
    counts = (jnp.sum(e0[:, None] == ids[None, :], axis=0)
              + jnp.sum(e1[:, None] == ids[None, :], axis=0)).astype(jnp.int32)
    tiles = (counts + tm - 1) // tm
    tile_end = jnp.cumsum(tiles)
    tile_start = tile_end - tiles
    pos0 = tile_start[e0] * tm + r0
    pos1 = tile_start[e1] * tm + r1
    tok = jnp.arange(t, dtype=jnp.int32)
    src = jnp.zeros((n_tiles * tm,), jnp.int32).at[pos0].set(tok).at[pos1].set(tok)
    ti = jnp.arange(n_tiles, dtype=jnp.int32)
    used = tile_end[-1]
    te = jnp.searchsorted(tile_end, jnp.minimum(ti, used - 1), side="right").astype(jnp.int32)
    te = jnp.minimum(te, N_EXPERTS - 1)
    tv = jnp.clip(counts[te] - (ti - tile_start[te]) * tm, 0, tm)
    tv = jnp.where(ti < used, tv, 0).astype(jnp.int32)
    return pos0, pos1, src.reshape(n_tiles, 1, tm), te, tv


def kernel(x, c, positions, ada_w, ada_b, norm_g, mla_w_in, mla_g_q, mla_g_kv, mla_w_uq,
           mla_w_ukv, mla_w_out, conv_w_pw1, conv_b_pw1, conv_w_dw, conv_b_dw, conv_ln_g,
           conv_ln_b, conv_w_pw2, conv_b_pw2, ffn_w_gate, ffn_w_up, ffn_w_down, moe_w_router,
           moe_w_gate, moe_w_up, moe_w_down, final_g):
    b, s, d = x.shape
    t = b * s
    f = ffn_w_gate.shape[-1]
    assert ada_w.shape[0] == 2 and mla_w_in.shape[0] == 1 and conv_w_pw1.shape[0] == 1
    assert moe_w_gate.shape[1] == N_EXPERTS and conv_w_dw.shape[1] == CONV_WIDTH

    scale = (QK_NOPE + QK_ROPE) ** -0.5
    w_in = jnp.pad(mla_w_in[0], ((0, 0), (0, LANES - QK_ROPE))).astype(BF16)
    wq = mla_w_uq[0].reshape(Q_LORA, MLA_HEADS, QK_NOPE + QK_ROPE) * scale
    wq = jnp.pad(wq, ((0, 0), (0, 0), (0, HEAD_PAD - QK_NOPE - QK_ROPE)))
    wq = wq.reshape(Q_LORA, MLA_HEADS * HEAD_PAD).astype(BF16)
    wkv = mla_w_ukv[0].reshape(KV_LORA, MLA_HEADS, QK_NOPE + V_HEAD)
    wk = wkv[:, :, :QK_NOPE].reshape(KV_LORA, MLA_HEADS * QK_NOPE).astype(BF16)
    wv = wkv[:, :, QK_NOPE:].reshape(KV_LORA, MLA_HEADS * V_HEAD).astype(BF16)
    w_out = mla_w_out[0].astype(BF16)
    w_pw1 = conv_w_pw1[0].astype(BF16)
    w_pw2 = conv_w_pw2[0].astype(BF16)
    wr = jnp.pad(moe_w_router[0], ((0, 0), (0, LANES - N_EXPERTS)))
    wr_hi = wr.astype(BF16)
    wr_lo = (wr - wr_hi.astype(F32)).astype(BF16)
    row = lambda v: v.reshape(1, -1)

    mods = _ada_call(c, ada_w.reshape(4, d, 3 * d), ada_b.reshape(4, 3 * d))
    mod = [mods[i].reshape(b, 1, 3 * d) for i in range(4)]

    q, k, v = _mla_front_call(x, positions.reshape(b, s, 1), mod[0], row(norm_g[0, 0]), w_in,
                              row(mla_g_q[0]), row(mla_g_kv[0]), wq, wk, wv, _rope_consts(),
                              tm=256)
    o = _attn_call(q, k, v, t=512, heads=2)
    x1, h2 = _attn_out_call(x, o, w_out, mod[0], mod[1], row(norm_g[0, 1]), tm=512)
    tm_d = 1024
    nt_d = t // tm_d
    y1 = _ffn_call(h2.reshape(t, d), ffn_w_gate.astype(BF16), ffn_w_up.astype(BF16),
                   ffn_w_down.astype(BF16), jnp.zeros((nt_d,), jnp.int32),
                   jnp.full((nt_d,), tm_d, jnp.int32), None, tm=tm_d, sub=tm_d, tf=512)

    x2, glu = _conv_front_call(x1, y1.reshape(b, s, d), mod[1], mod[2], row(norm_g[1, 0]),
                               w_pw1, row(conv_b_pw1[0]), tm=256)
    x3, h4, route = _conv_back_call(glu, x2, mod[2], mod[3], conv_w_dw[0], row(conv_b_dw[0]),
                                    row(conv_ln_g[0]), row(conv_ln_b[0]), w_pw2,
                                    row(conv_b_pw2[0]), row(norm_g[1, 1]), wr_hi, wr_lo, tm=256)
    route = route.reshape(t, LANES)
    tm_e = 1024
    nt_e = (2 * t) // tm_e + N_EXPERTS
    pos0, pos1, src, te, tv = _moe_plan(route, tm_e, nt_e)
    ys = _ffn_call(h4.reshape(t, d), moe_w_gate[0], moe_w_up[0], moe_w_down[0], te, tv, src,
                   tm=tm_e, sub=512, tf=256)
    tm_c = 256
    out = _combine_call(pos0.reshape(t // tm_c, 1, tm_c), pos1.reshape(t // tm_c, 1, tm_c),
                        x3.reshape(t, d), route, mod[3], row(final_g), ys, tm=tm_c,
                        tiles_per_batch=s // tm_c)
    return out.reshape(b, s, d)
```

```python
import functools

import numpy as np
import jax
import jax.numpy as jnp
from jax import lax
from jax.experimental import pallas as pl
from jax.experimental.pallas import tpu as pltpu

F32 = jnp.float32
BF16 = jnp.bfloat16

MLA_HEADS = 16
Q_LORA = 512
KV_LORA = 512
QK_NOPE = 128
QK_ROPE = 64
V_HEAD = 128
ROPE_THETA = 10000.0
CONV_WIDTH = 31
N_EXPERTS = 8
EPS = 1e-6

LANES = 128
SUBLANES = 8
HEAD_PAD = 2 * LANES
HALO = 32
VMEM_LIMIT = 56 * 1024 * 1024


def _cparams(sem):
    return pltpu.CompilerParams(dimension_semantics=sem, vmem_limit_bytes=VMEM_LIMIT)


def _rms(x, g):
    return x * lax.rsqrt(jnp.mean(x * x, axis=-1, keepdims=True) + EPS) * g


def _adaln(x, g, mod, d):
    return _rms(x, g) * (1.0 + mod[:, d:2 * d]) + mod[:, 0:d]


def _dot(a, b):
    return jnp.dot(a, b, preferred_element_type=F32)


def _ada_kernel(c_ref, w_ref, b_ref, o_ref):
    c = c_ref[...]
    sc = (c * jax.nn.sigmoid(c)).astype(BF16)
    o_ref[0] = _dot(sc, w_ref[0].astype(BF16)) + b_ref[0]


def _ada_call(c, ada_w, ada_b):
    n, d, d3 = ada_w.shape
    b = c.shape[0]
    tn = 1024
    return pl.pallas_call(
        _ada_kernel,
        grid=(n, d3 // tn),
        in_specs=[
            pl.BlockSpec((b, d), lambda i, j: (0, 0)),
            pl.BlockSpec((1, d, tn), lambda i, j: (i, 0, j)),
            pl.BlockSpec((1, 1, tn), lambda i, j: (i, 0, j)),
        ],
        out_specs=pl.BlockSpec((1, b, tn), lambda i, j: (i, 0, j)),
        out_shape=jax.ShapeDtypeStruct((n, b, d3), F32),
        compiler_params=_cparams(("arbitrary", "arbitrary")),
        name="ada",
    )(c, ada_w, ada_b.reshape(n, 1, d3))


def _rope_tile(t, cosv, sa, sb):
    return t * cosv + pltpu.roll(t, 32, 1) * sa + pltpu.roll(t, 96, 1) * sb


def _mla_front_kernel(x_ref, pos_ref, mod_ref, g_ref, win_ref, gq_ref, gkv_ref,
                      wq_ref, wk_ref, wv_ref, rc_ref, q_ref, k_ref, v_ref, *, d):
    x = x_ref[0]
    h = _adaln(x, g_ref[...], mod_ref[0], d).astype(BF16)
    lat = _dot(h, win_ref[...])
    cq = _rms(lat[:, 0:Q_LORA], gq_ref[...]).astype(BF16)
    ckv = _rms(lat[:, Q_LORA:Q_LORA + KV_LORA], gkv_ref[...]).astype(BF16)
    kr = lat[:, Q_LORA + KV_LORA:Q_LORA + KV_LORA + LANES]

    ang = pos_ref[0].astype(F32) * rc_ref[0:1, :]
    cosv = jnp.cos(ang)
    sinv = jnp.sin(ang)
    sa = sinv * rc_ref[1:2, :]
    sb = sinv * rc_ref[2:3, :]

    q = _dot(cq, wq_ref[...])
    for hh in range(MLA_HEADS):
        lo = hh * HEAD_PAD
        q_ref[0, :, lo:lo + LANES] = q[:, lo:lo + LANES].astype(BF16)
        q_ref[0, :, lo + LANES:lo + HEAD_PAD] = _rope_tile(
            q[:, lo + LANES:lo + HEAD_PAD], cosv, sa, sb).astype(BF16)

    kn = _dot(ckv, wk_ref[...])
    krr = _rope_tile(kr, cosv, sa, sb).astype(BF16)
    for hh in range(MLA_HEADS):
        lo = hh * HEAD_PAD
        k_ref[0, :, lo:lo + LANES] = kn[:, hh * LANES:(hh + 1) * LANES].astype(BF16)
        k_ref[0, :, lo + LANES:lo + HEAD_PAD] = krr
    v_ref[0] = _dot(ckv, wv_ref[...]).astype(BF16)


def _mla_front_call(x, pos3, mod, g, win, gq, gkv, wq, wk, wv, rc, tm):
    b, s, d = x.shape
    hq = MLA_HEADS * HEAD_PAD
    hv = MLA_HEADS * V_HEAD
    const = lambda shape: pl.BlockSpec(shape, lambda bi, si: (0,) * len(shape))
    return pl.pallas_call(
        functools.partial(_mla_front_kernel, d=d),
        grid=(b, s // tm),
        in_specs=[
            pl.BlockSpec((1, tm, d), lambda bi, si: (bi, si, 0)),
            pl.BlockSpec((1, tm, 1), lambda bi, si: (bi, si, 0)),
            pl.BlockSpec((1, 1, 3 * d), lambda bi, si: (bi, 0, 0)),
            const(g.shape), const(win.shape), const(gq.shape), const(gkv.shape),
            const(wq.shape), const(wk.shape), const(wv.shape), const(rc.shape),
        ],
        out_specs=[
            pl.BlockSpec((1, tm, hq), lambda bi, si: (bi, si, 0)),
            pl.BlockSpec((1, tm, hq), lambda bi, si: (bi, si, 0)),
            pl.BlockSpec((1, tm, hv), lambda bi, si: (bi, si, 0)),
        ],
        out_shape=[
            jax.ShapeDtypeStruct((b, s, hq), BF16),
            jax.ShapeDtypeStruct((b, s, hq), BF16),
            jax.ShapeDtypeStruct((b, s, hv), BF16),
        ],
        compiler_params=_cparams(("arbitrary", "arbitrary")),
        name="mla_front",
    )(x, pos3, mod, g, win, gq, gkv, wq, wk, wv, rc)


def _attn_kernel(q_ref, k_ref, v_ref, o_ref, *, t, heads):
    qi = pl.program_id(2)
    tril = (lax.broadcasted_iota(jnp.int32, (t, t), 1)
            <= lax.broadcasted_iota(jnp.int32, (t, t), 0))

    def block(j, carry, masked):
        start = pl.multiple_of(j * t, t)
        out = []
        for hh in range(heads):
            m, l, acc = carry[hh]
            q = q_ref[0, :, hh * HEAD_PAD:(hh + 1) * HEAD_PAD]
            k = k_ref[0, pl.ds(start, t), hh * HEAD_PAD:(hh + 1) * HEAD_PAD]
            v = v_ref[0, pl.ds(start, t), hh * V_HEAD:(hh + 1) * V_HEAD]
            s = lax.dot_general(q, k, (((1,), (1,)), ((), ())), preferred_element_type=F32)
            if masked:
                s = jnp.where(tril, s, -jnp.inf)
            m_new = jnp.maximum(m, jnp.max(s, axis=-1, keepdims=True))
            alpha = jnp.exp(m - m_new)
            p = jnp.exp(s - m_new)
            l = alpha * l + jnp.sum(p, axis=-1, keepdims=True)
            acc = alpha * acc + _dot(p.astype(BF16), v)
            out.append((m_new, l, acc))
        return tuple(out)

    init = tuple((jnp.full((t, 1), -jnp.inf, F32), jnp.zeros((t, 1), F32),
                  jnp.zeros((t, V_HEAD), F32)) for _ in range(heads))
    carry = lax.fori_loop(0, qi, functools.partial(block, masked=False), init)
    carry = block(qi, carry, True)
    for hh in range(heads):
        _, l, acc = carry[hh]
        o_ref[0, :, hh * V_HEAD:(hh + 1) * V_HEAD] = (acc / l).astype(BF16)


def _attn_call(q, k, v, t, heads):
    b, s, _ = q.shape
    return pl.pallas_call(
        functools.partial(_attn_kernel, t=t, heads=heads),
        grid=(b, MLA_HEADS // heads, s // t),
        in_specs=[
            pl.BlockSpec((1, t, heads * HEAD_PAD), lambda bi, hi, qi: (bi, qi, hi)),
            pl.BlockSpec((1, s, heads * HEAD_PAD), lambda bi, hi, qi: (bi, 0, hi)),
            pl.BlockSpec((1, s, heads * V_HEAD), lambda bi, hi, qi: (bi, 0, hi)),
        ],
        out_specs=pl.BlockSpec((1, t, heads * V_HEAD), lambda bi, hi, qi: (bi, qi, hi)),
        out_shape=jax.ShapeDtypeStruct((b, s, MLA_HEADS * V_HEAD), BF16),
        compiler_params=_cparams(("arbitrary", "arbitrary", "arbitrary")),
        name="attention",
    )(q, k, v)


def _attn_out_kernel(x_ref, o_ref, w_ref, moda_ref, modb_ref, g_ref, x1_ref, h_ref, *, d):
    y = _dot(o_ref[0], w_ref[...])
    x1 = x_ref[0] + moda_ref[0][:, 2 * d:3 * d] * y
    x1_ref[0] = x1
    h_ref[0] = _adaln(x1, g_ref[...], modb_ref[0], d).astype(BF16)


def _attn_out_call(x, o, w, moda, modb, g, tm):
    b, s, d = x.shape
    tile = lambda: pl.BlockSpec((1, tm, d), lambda bi, si: (bi, si, 0))
    modspec = lambda: pl.BlockSpec((1, 1, 3 * d), lambda bi, si: (bi, 0, 0))
    return pl.pallas_call(
        functools.partial(_attn_out_kernel, d=d),
        grid=(b, s // tm),
        in_specs=[tile(), tile(), pl.BlockSpec(w.shape, lambda bi, si: (0, 0)),
                  modspec(), modspec(), pl.BlockSpec(g.shape, lambda bi, si: (0, 0))],
        out_specs=[tile(), tile()],
        out_shape=[jax.ShapeDtypeStruct((b, s, d), F32), jax.ShapeDtypeStruct((b, s, d), BF16)],
        compiler_params=_cparams(("arbitrary", "arbitrary")),
        name="attn_out",
    )(x, o, w, moda, modb, g)


def _ffn_body(valid, h_bf, wg_ref, wu_ref, wd_ref, o_ref, tm, sub, first_block_extra=None):
    for sb in range(tm // sub):
        @pl.when(sb * sub < valid)
        def _():
            if sb == 0 and first_block_extra is not None:
                first_block_extra()
            rows = slice(sb * sub, (sb + 1) * sub)
            h = h_bf[rows, :]
            g = _dot(h, wg_ref[0].astype(BF16))
            u = _dot(h, wu_ref[0].astype(BF16))
            a = (g * jax.nn.sigmoid(g) * u).astype(BF16)
            o_ref[rows, :] += _dot(a, wd_ref[0].astype(BF16))


def _ffn_dense_kernel(te_ref, tv_ref, h_ref, wg_ref, wu_ref, wd_ref, o_ref, *, tm, sub):
    i = pl.program_id(0)
    j = pl.program_id(1)

    @pl.when(j == 0)
    def _():
        o_ref[...] = jnp.zeros_like(o_ref)

    _ffn_body(tv_ref[i], h_ref, wg_ref, wu_ref, wd_ref, o_ref, tm, sub)


def _ffn_gather_kernel(te_ref, tv_ref, src_ref, srcn_ref, h_hbm, wg_ref, wu_ref, wd_ref, o_ref,
                       hf_ref, hb_ref, sem, *, tm, sub, rows_per_step):
    i = pl.program_id(0)
    j = pl.program_id(1)
    valid = tv_ref[i]
    n_rows = hf_ref.shape[0]

    def row_copy(idx_ref, r):
        return pltpu.make_async_copy(h_hbm.at[pl.ds(idx_ref[0, 0, r], 1), :],
                                     hf_ref.at[pl.ds(r, 1), :], sem)

    def wait_rows():
        pltpu.make_async_copy(h_hbm.at[pl.ds(0, n_rows), :], hf_ref, sem).wait()

    @pl.when(j == 0)
    def _():
        o_ref[...] = jnp.zeros_like(o_ref)

        @pl.when(jnp.logical_and(i == 0, valid > 0))
        def _():
            def issue(r, c):
                row_copy(src_ref, r).start()
                return c
            lax.fori_loop(0, n_rows, issue, 0)
            wait_rows()

        @pl.when(jnp.logical_and(i > 0, tv_ref[jnp.maximum(i - 1, 0)] > 0))
        def _():
            wait_rows()

        @pl.when(valid > 0)
        def _():
            hb_ref[...] = hf_ref[0:tm, :].astype(BF16)

    def request_next_rows():
        for q in range(rows_per_step):
            row_copy(srcn_ref, j * rows_per_step + q).start()

    _ffn_body(valid, hb_ref, wg_ref, wu_ref, wd_ref, o_ref, tm, sub, request_next_rows)


def _ffn_call(h, wg, wu, wd, tile_expert, tile_valid, src, tm, sub, tf):
    d = wg.shape[1]
    f = wg.shape[2]
    nf = f // tf
    nt = tile_expert.shape[0]

    def jj(i, j, tv):
        return jnp.where(tv[i] > 0, j, nf - 1)

    w_specs = [
        pl.BlockSpec((1, d, tf), lambda i, j, te, tv: (te[i], 0, jj(i, j, tv))),
        pl.BlockSpec((1, d, tf), lambda i, j, te, tv: (te[i], 0, jj(i, j, tv))),
        pl.BlockSpec((1, tf, d), lambda i, j, te, tv: (te[i], jj(i, j, tv), 0)),
    ]
    out_spec = pl.BlockSpec((tm, d), lambda i, j, te, tv: (i, 0))
    out_shape = jax.ShapeDtypeStruct((nt * tm, d), F32)
    if src is None:
        grid_spec = pltpu.PrefetchScalarGridSpec(
            num_scalar_prefetch=2, grid=(nt, nf),
            in_specs=[pl.BlockSpec((tm, d), lambda i, j, te, tv: (i, 0))] + w_specs,
            out_specs=out_spec)
        return pl.pallas_call(
            functools.partial(_ffn_dense_kernel, tm=tm, sub=sub), grid_spec=grid_spec,
            out_shape=out_shape,
            compiler_params=_cparams(("arbitrary", "arbitrary")), name="ffn_dense",
        )(tile_expert, tile_valid, h, wg, wu, wd)
    n_rows = src.shape[2]
    assert n_rows % nf == 0 and n_rows >= tm
    grid_spec = pltpu.PrefetchScalarGridSpec(
        num_scalar_prefetch=2, grid=(nt, nf),
        in_specs=[pl.BlockSpec((1, 1, n_rows), lambda i, j, te, tv: (i, 0, 0),
                               memory_space=pltpu.SMEM),
                  pl.BlockSpec((1, 1, n_rows),
                               lambda i, j, te, tv: (jnp.minimum(i + 1, nt - 1), 0, 0),
                               memory_space=pltpu.SMEM),
                  pl.BlockSpec(memory_space=pl.ANY)] + w_specs,
        out_specs=out_spec,
        scratch_shapes=[pltpu.VMEM((n_rows, d), F32), pltpu.VMEM((tm, d), BF16),
                        pltpu.SemaphoreType.DMA(())])
    return pl.pallas_call(
        functools.partial(_ffn_gather_kernel, tm=tm, sub=sub, rows_per_step=n_rows // nf),
        grid_spec=grid_spec, out_shape=out_shape,
        compiler_params=_cparams(("arbitrary", "arbitrary")), name="ffn_moe",
    )(tile_expert, tile_valid, src, src, h, wg, wu, wd)


def _conv_front_kernel(x1_ref, y_ref, moda_ref, modb_ref, g_ref, w_ref, b_ref,
                       x2_ref, glu_ref, *, d):
    x2 = x1_ref[0] + moda_ref[0][:, 2 * d:3 * d] * y_ref[0]
    x2_ref[0] = x2
    h = _adaln(x2, g_ref[...], modb_ref[0], d).astype(BF16)
    u = _dot(h, w_ref[...]) + b_ref[...]
    glu_ref[0] = u[:, 0:d] * jax.nn.sigmoid(u[:, d:2 * d])


def _conv_front_call(x1, y, moda, modb, g, w, bias, tm):
    b, s, d = x1.shape
    tile = lambda: pl.BlockSpec((1, tm, d), lambda bi, si: (bi, si, 0))
    modspec = lambda: pl.BlockSpec((1, 1, 3 * d), lambda bi, si: (bi, 0, 0))
    const = lambda a: pl.BlockSpec(a.shape, lambda bi, si: (0,) * a.ndim)
    return pl.pallas_call(
        functools.partial(_conv_front_kernel, d=d),
        grid=(b, s // tm),
        in_specs=[tile(), tile(), modspec(), modspec(), const(g), const(w), const(bias)],
        out_specs=[tile(), tile()],
        out_shape=[jax.ShapeDtypeStruct((b, s, d), F32), jax.ShapeDtypeStruct((b, s, d), F32)],
        compiler_params=_cparams(("arbitrary", "arbitrary")),
        name="conv_front",
    )(x1, y, moda, modb, g, w, bias)


def _conv_back_kernel(glu_ref, halo_ref, x2_ref, moda_ref, modb_ref, wdw_ref, bdw_ref,
                      lng_ref, lnb_ref, w2_ref, b2_ref, g_ref, wrh_ref, wrl_ref,
                      x3_ref, h4_ref, route_ref, ext_ref, conv_ref, base_ref, sh_ref,
                      *, d, tm, rows, cols):
    bi = pl.program_id(0)
    si = pl.program_id(1)

    @pl.when(jnp.logical_and(bi == 0, si == 0))
    def _():
        base_ref[...] = jnp.zeros_like(base_ref)

    ext_ref[0:HALO, :] = jnp.where(si == 0, 0.0, halo_ref[0])
    ext_ref[HALO:HALO + tm, :] = glu_ref[0]

    off = HALO - (CONV_WIDTH - 1)
    span = sh_ref.shape[1]
    for c0 in range(0, d, cols):
        for r in range(1, SUBLANES):
            sh_ref[r - 1] = ext_ref[r:r + span, c0:c0 + cols]
        for r0 in range(0, tm, rows):
            groups = (rows // SUBLANES, SUBLANES, cols)
            acc = jnp.zeros(groups, F32) + bdw_ref[:, c0:c0 + cols][None]
            for k in range(CONV_WIDTH):
                r = (off + k) % SUBLANES
                a = r0 + off + k - r
                if r == 0:
                    tap = ext_ref[a:a + rows, c0:c0 + cols]
                else:
                    tap = sh_ref[r - 1, a:a + rows, :]
                acc = acc + wdw_ref[k, :, c0:c0 + cols][None] * tap.reshape(groups)
            conv_ref[r0:r0 + rows, c0:c0 + cols] = acc.reshape(rows, cols)

    u = conv_ref[...]
    mu = jnp.mean(u, axis=-1, keepdims=True)
    uc = u - mu
    var = jnp.mean(uc * uc, axis=-1, keepdims=True)
    u = uc * lax.rsqrt(var + EPS) * lng_ref[...] + lnb_ref[...]
    u = (u * jax.nn.sigmoid(u)).astype(BF16)
    y = _dot(u, w2_ref[...]) + b2_ref[...]
    x3 = x2_ref[0] + moda_ref[0][:, 2 * d:3 * d] * y
    x3_ref[0] = x3
    h4 = _adaln(x3, g_ref[...], modb_ref[0], d)
    h4_ref[0] = h4

    hh = h4.astype(BF16)
    hl = (h4 - hh.astype(F32)).astype(BF16)
    logits = _dot(hh, wrh_ref[...]) + _dot(hl, wrh_ref[...]) + _dot(hh, wrl_ref[...])
    lane = lax.broadcasted_iota(jnp.int32, (tm, LANES), 1).astype(F32)
    lg = jnp.where(lane < N_EXPERTS, logits, -jnp.inf)
    m1 = jnp.max(lg, axis=-1, keepdims=True)
    i1 = jnp.min(jnp.where(lg == m1, lane, float(LANES)), axis=-1, keepdims=True)
    lg2 = jnp.where(lane == i1, -jnp.inf, lg)
    m2 = jnp.max(lg2, axis=-1, keepdims=True)
    i2 = jnp.min(jnp.where(lg2 == m2, lane, float(LANES)), axis=-1, keepdims=True)
    e2 = jnp.exp(m2 - m1)
    w0 = 1.0 / (1.0 + e2)
    w1 = e2 / (1.0 + e2)
    oh0 = (lane == i1).astype(F32)
    oh1 = (lane == i2).astype(F32)
    both = oh0 + oh1
    tri = (lax.broadcasted_iota(jnp.int32, (tm, tm), 0)
           > lax.broadcasted_iota(jnp.int32, (tm, tm), 1)).astype(BF16)
    cnt = _dot(tri, both.astype(BF16)) + base_ref[...]
    r0 = jnp.sum(cnt * oh0, axis=-1, keepdims=True)
    r1 = jnp.sum(cnt * oh1, axis=-1, keepdims=True)
    base_ref[...] = base_ref[...] + jnp.sum(both, axis=0, keepdims=True)
    route = jnp.where(lane == 0.0, i1, 0.0)
    route = jnp.where(lane == 1.0, i2, route)
    route = jnp.where(lane == 2.0, w0, route)
    route = jnp.where(lane == 3.0, w1, route)
    route = jnp.where(lane == 4.0, r0, route)
    route = jnp.where(lane == 5.0, r1, route)
    route_ref[0] = route


def _conv_back_call(glu, x2, moda, modb, wdw, bdw, lng, lnb, w2, b2, g, wrh, wrl, tm):
    b, s, d = x2.shape
    tile = lambda: pl.BlockSpec((1, tm, d), lambda bi, si: (bi, si, 0))
    modspec = lambda: pl.BlockSpec((1, 1, 3 * d), lambda bi, si: (bi, 0, 0))
    const = lambda a: pl.BlockSpec(a.shape, lambda bi, si: (0,) * a.ndim)
    hpt = tm // HALO
    halo = pl.BlockSpec((1, HALO, d), lambda bi, si: (bi, jnp.maximum(si * hpt - 1, 0), 0))
    cols = 2 * LANES
    return pl.pallas_call(
        functools.partial(_conv_back_kernel, d=d, tm=tm, rows=64, cols=cols),
        grid=(b, s // tm),
        in_specs=[tile(), halo, tile(), modspec(), modspec(), const(wdw), const(bdw),
                  const(lng), const(lnb), const(w2), const(b2), const(g), const(wrh), const(wrl)],
        out_specs=[tile(), tile(), pl.BlockSpec((1, tm, LANES), lambda bi, si: (bi, si, 0))],
        out_shape=[jax.ShapeDtypeStruct((b, s, d), F32), jax.ShapeDtypeStruct((b, s, d), F32),
                   jax.ShapeDtypeStruct((b, s, LANES), F32)],
        scratch_shapes=[pltpu.VMEM((tm + HALO, d), F32), pltpu.VMEM((tm, d), F32),
                        pltpu.VMEM((1, LANES), F32),
                        pltpu.VMEM((SUBLANES - 1, tm + HALO - SUBLANES, cols), F32)],
        compiler_params=_cparams(("arbitrary", "arbitrary")),
        name="conv_back",
    )(glu, glu, x2, moda, modb, wdw, bdw, lng, lnb, w2, b2, g, wrh, wrl)


def _combine_kernel(p0_ref, p1_ref, x_ref, route_ref, mod_ref, fg_ref, ys_hbm, o_ref,
                    y0_ref, y1_ref, sem, *, d, tm):
    def copies(r):
        return (pltpu.make_async_copy(ys_hbm.at[pl.ds(p0_ref[0, 0, r], 1), :],
                                      y0_ref.at[pl.ds(r, 1), :], sem.at[0]),
                pltpu.make_async_copy(ys_hbm.at[pl.ds(p1_ref[0, 0, r], 1), :],
                                      y1_ref.at[pl.ds(r, 1), :], sem.at[1]))

    def issue(r, c):
        a, b = copies(r)
        a.start()
        b.start()
        return c

    lax.fori_loop(0, tm, issue, 0, unroll=8)
    pltpu.make_async_copy(ys_hbm.at[pl.ds(0, tm), :], y0_ref, sem.at[0]).wait()
    pltpu.make_async_copy(ys_hbm.at[pl.ds(0, tm), :], y1_ref, sem.at[1]).wait()
    w0 = route_ref[:, 2:3]
    w1 = route_ref[:, 3:4]
    y = w0 * y0_ref[...] + w1 * y1_ref[...]
    x = x_ref[...] + mod_ref[0][:, 2 * d:3 * d] * y
    o_ref[...] = _rms(x, fg_ref[...])


def _combine_call(p0, p1, x3, route, mod, fg, ys, tm, tiles_per_batch):
    t, d = x3.shape
    return pl.pallas_call(
        functools.partial(_combine_kernel, d=d, tm=tm),
        grid=(t // tm,),
        in_specs=[
            pl.BlockSpec((1, 1, tm), lambda i: (i, 0, 0), memory_space=pltpu.SMEM),
            pl.BlockSpec((1, 1, tm), lambda i: (i, 0, 0), memory_space=pltpu.SMEM),
            pl.BlockSpec((tm, d), lambda i: (i, 0)),
            pl.BlockSpec((tm, LANES), lambda i: (i, 0)),
            pl.BlockSpec((1, 1, 3 * d), lambda i: (i // tiles_per_batch, 0, 0)),
            pl.BlockSpec(fg.shape, lambda i: (0, 0)),
            pl.BlockSpec(memory_space=pl.ANY),
        ],
        out_specs=pl.BlockSpec((tm, d), lambda i: (i, 0)),
        out_shape=jax.ShapeDtypeStruct((t, d), F32),
        scratch_shapes=[pltpu.VMEM((tm, d), F32), pltpu.VMEM((tm, d), F32),
                        pltpu.SemaphoreType.DMA((2,))],
        compiler_params=_cparams(("arbitrary",)),
        name="combine",
    )(p0, p1, x3, route, mod, fg, ys)


def _rope_consts():
    lane = np.arange(LANES)
    half = QK_ROPE // 2
    freq = np.where(lane < QK_ROPE, ROPE_THETA ** (-(2.0 * (lane % half)) / QK_ROPE), 0.0)
    sa = np.where((lane >= half) & (lane < QK_ROPE), 1.0, 0.0)
    sb = np.where(lane < half, -1.0, 0.0)
    rc = np.zeros((8, LANES), np.float32)
    rc[0], rc[1], rc[2] = freq, sa, sb
    return jnp.asarray(rc)


def _moe_plan(route, tm, n_tiles, n_rows):
    t = route.shape[0]
    assert n_tiles == (2 * t) // tm + N_EXPERTS and (2 * t) % tm == 0
    e0 = route[:, 0].astype(jnp.int32)
    e1 = route[:, 1].astype(jnp.int32)
    r0 = route[:, 4].astype(jnp.int32)
    r1 = route[:, 5].astype(jnp.int32)
    ids = jnp.arange(N_EXPERTS, dtype=jnp.int32)
    counts = (jnp.sum(e0[:, None] == ids[None, :], axis=0)
              + jnp.sum(e1[:, None] == ids[None, :], axis=0)).astype(jnp.int32)
    tiles = (counts + tm - 1) // tm
    tile_end = jnp.cumsum(tiles)
    tile_start = tile_end - tiles
    pos0 = tile_start[e0] * tm + r0
    pos1 = tile_start[e1] * tm + r1
    tok = jnp.arange(t, dtype=jnp.int32)
    pos = jnp.concatenate([pos0, pos1])
    src = jnp.zeros((n_tiles * n_rows,), jnp.int32).at[(pos // tm) * n_rows + pos % tm].set(
        jnp.concatenate([tok, tok]))
    ti = jnp.arange(n_tiles, dtype=jnp.int32)
    used = tile_end[-1]
    te = jnp.searchsorted(tile_end, jnp.minimum(ti, used - 1), side="right").astype(jnp.int32)
    te = jnp.minimum(te, N_EXPERTS - 1)
    tv = jnp.clip(counts[te] - (ti - tile_start[te]) * tm, 0, tm)
    tv = jnp.where(ti < used, tv, 0).astype(jnp.int32)
    return pos0, pos1, src.reshape(n_tiles, 1, n_rows), te, tv


def kernel(x, c, positions, ada_w, ada_b, norm_g, mla_w_in, mla_g_q, mla_g_kv, mla_w_uq,
           mla_w_ukv, mla_w_out, conv_w_pw1, conv_b_pw1, conv_w_dw, conv_b_dw, conv_ln_g,
           conv_ln_b, conv_w_pw2, conv_b_pw2, ffn_w_gate, ffn_w_up, ffn_w_down, moe_w_router,
           moe_w_gate, moe_w_up, moe_w_down, final_g):
    b, s, d = x.shape
    t = b * s
    f = ffn_w_gate.shape[-1]
    assert ada_w.shape[0] == 2 and mla_w_in.shape[0] == 1 and conv_w_pw1.shape[0] == 1
    assert moe_w_gate.shape[1] == N_EXPERTS and conv_w_dw.shape[1] == CONV_WIDTH

    scale = (QK_NOPE + QK_ROPE) ** -0.5
    w_in = jnp.pad(mla_w_in[0], ((0, 0), (0, LANES - QK_ROPE))).astype(BF16)
    wq = mla_w_uq[0].reshape(Q_LORA, MLA_HEADS, QK_NOPE + QK_ROPE) * scale
    wq = jnp.pad(wq, ((0, 0), (0, 0), (0, HEAD_PAD - QK_NOPE - QK_ROPE)))
    wq = wq.reshape(Q_LORA, MLA_HEADS * HEAD_PAD).astype(BF16)
    wkv = mla_w_ukv[0].reshape(KV_LORA, MLA_HEADS, QK_NOPE + V_HEAD)
    wk = wkv[:, :, :QK_NOPE].reshape(KV_LORA, MLA_HEADS * QK_NOPE).astype(BF16)
    wv = wkv[:, :, QK_NOPE:].reshape(KV_LORA, MLA_HEADS * V_HEAD).astype(BF16)
    w_out = mla_w_out[0].astype(BF16)
    w_pw1 = conv_w_pw1[0].astype(BF16)
    w_pw2 = conv_w_pw2[0].astype(BF16)
    wr = jnp.pad(moe_w_router[0], ((0, 0), (0, LANES - N_EXPERTS)))
    wr_hi = wr.astype(BF16)
    wr_lo = (wr - wr_hi.astype(F32)).astype(BF16)
    row = lambda v: v.reshape(1, -1)

    mods = _ada_call(c, ada_w.reshape(4, d, 3 * d), ada_b.reshape(4, 3 * d))
    mod = [mods[i].reshape(b, 1, 3 * d) for i in range(4)]

    q, k, v = _mla_front_call(x, positions.reshape(b, s, 1), mod[0], row(norm_g[0, 0]), w_in,
                              row(mla_g_q[0]), row(mla_g_kv[0]), wq, wk, wv, _rope_consts(),
                              tm=256)
    o = _attn_call(q, k, v, t=512, heads=4)
    x1, h2 = _attn_out_call(x, o, w_out, mod[0], mod[1], row(norm_g[0, 1]), tm=512)
    tm_d = 1024
    nt_d = t // tm_d
    y1 = _ffn_call(h2.reshape(t, d), ffn_w_gate.astype(BF16), ffn_w_up.astype(BF16),
                   ffn_w_down.astype(BF16), jnp.zeros((nt_d,), jnp.int32),
                   jnp.full((nt_d,), tm_d, jnp.int32), None, tm=tm_d, sub=tm_d, tf=512)

    x2, glu = _conv_front_call(x1, y1.reshape(b, s, d), mod[1], mod[2], row(norm_g[1, 0]),
                               w_pw1, row(conv_b_pw1[0]), tm=256)
    w_dw8 = jnp.broadcast_to(conv_w_dw[0][:, None, :], (CONV_WIDTH, SUBLANES, d))
    b_dw8 = jnp.broadcast_to(row(conv_b_dw[0]), (SUBLANES, d))
    x3, h4, route = _conv_back_call(glu, x2, mod[2], mod[3], w_dw8, b_dw8,
                                    row(conv_ln_g[0]), row(conv_ln_b[0]), w_pw2,
                                    row(conv_b_pw2[0]), row(norm_g[1, 1]), wr_hi, wr_lo, tm=256)
    route = route.reshape(t, LANES)
    tm_e = 1024
    tf_e = 256
    nt_e = (2 * t) // tm_e + N_EXPERTS
    nf_e = f // tf_e
    rps = -(-tm_e // nf_e)
    while (rps * nf_e) % SUBLANES:
        rps += 1
    rows_e = rps * nf_e
    pos0, pos1, src, te, tv = _moe_plan(route, tm_e, nt_e, rows_e)
    ys = _ffn_call(h4.reshape(t, d), moe_w_gate[0], moe_w_up[0], moe_w_down[0], te, tv, src,
                   tm=tm_e, sub=512, tf=tf_e)
    tm_c = 256
    out = _combine_call(pos0.reshape(t // tm_c, 1, tm_c), pos1.reshape(t // tm_c, 1, tm_c),
                        x3.reshape(t, d), route, mod[3], row(final_g), ys, tm=tm_c,
                        tiles_per_batch=s // tm_c)
    return out.reshape(b, s, d)
```

```python
import functools

import numpy as np
import jax
import jax.numpy as jnp
from jax import lax
from jax.experimental import pallas as pl
from jax.experimental.pallas import tpu as pltpu

F32 = jnp.float32
BF16 = jnp.bfloat16

MLA_HEADS = 16
Q_LORA = 512
KV_LORA = 512
QK_NOPE = 128
QK_ROPE = 64
V_HEAD = 128
ROPE_THETA = 10000.0
CONV_WIDTH = 31
N_EXPERTS = 8
EPS = 1e-6

LANES = 128
SUBLANES = 8
HEAD_PAD = 2 * LANES
HALO = 32
VMEM_LIMIT = 56 * 1024 * 1024


def _cparams(sem):
    return pltpu.CompilerParams(dimension_semantics=sem, vmem_limit_bytes=VMEM_LIMIT)


def _rms(x, g):
    return x * lax.rsqrt(jnp.mean(x * x, axis=-1, keepdims=True) + EPS) * g


def _adaln(x, g, mod, d):
    return _rms(x, g) * (1.0 + mod[:, d:2 * d]) + mod[:, 0:d]


def _dot(a, b):
    return jnp.dot(a, b, preferred_element_type=F32)


def _mod_spec(mod, d, batch_of):
    k = mod[1]
    return pl.BlockSpec((None, 1, 1, 3 * d), lambda *g: (k, batch_of(*g), 0, 0))


def _ada_kernel(c_ref, w_ref, b_ref, o_ref):
    c = c_ref[...]
    sc = (c * jax.nn.sigmoid(c)).astype(BF16)
    o_ref[0] = _dot(sc, w_ref[0].astype(BF16)) + b_ref[0]


def _ada_call(c, ada_w, ada_b):
    n, d, d3 = ada_w.shape
    b = c.shape[0]
    tn = 1024
    return pl.pallas_call(
        _ada_kernel,
        grid=(n, d3 // tn),
        in_specs=[
            pl.BlockSpec((b, d), lambda i, j: (0, 0)),
            pl.BlockSpec((1, d, tn), lambda i, j: (i, 0, j)),
            pl.BlockSpec((1, 1, tn), lambda i, j: (i, 0, j)),
        ],
        out_specs=pl.BlockSpec((1, b, tn), lambda i, j: (i, 0, j)),
        out_shape=jax.ShapeDtypeStruct((n, b, d3), F32),
        compiler_params=_cparams(("arbitrary", "arbitrary")),
        name="ada",
    )(c, ada_w, ada_b.reshape(n, 1, d3))


def _rope_tile(t, cosv, sa, sb):
    return t * cosv + pltpu.roll(t, 32, 1) * sa + pltpu.roll(t, 96, 1) * sb


def _mla_front_kernel(x_ref, pos_ref, mod_ref, g_ref, win_ref, gq_ref, gkv_ref,
                      wq_ref, wk_ref, wv_ref, rc_ref, q_ref, k_ref, v_ref, *, d):
    x = x_ref[0]
    h = _adaln(x, g_ref[...], mod_ref[0], d).astype(BF16)
    lat = _dot(h, win_ref[...])
    cq = _rms(lat[:, 0:Q_LORA], gq_ref[...]).astype(BF16)
    ckv = _rms(lat[:, Q_LORA:Q_LORA + KV_LORA], gkv_ref[...]).astype(BF16)
    kr = lat[:, Q_LORA + KV_LORA:Q_LORA + KV_LORA + LANES]

    ang = pos_ref[0].astype(F32) * rc_ref[0:1, :]
    cosv = jnp.cos(ang)
    sinv = jnp.sin(ang)
    sa = sinv * rc_ref[1:2, :]
    sb = sinv * rc_ref[2:3, :]

    q = _dot(cq, wq_ref[...])
    for hh in range(MLA_HEADS):
        lo = hh * HEAD_PAD
        q_ref[0, :, lo:lo + LANES] = q[:, lo:lo + LANES].astype(BF16)
        q_ref[0, :, lo + LANES:lo + HEAD_PAD] = _rope_tile(
            q[:, lo + LANES:lo + HEAD_PAD], cosv, sa, sb).astype(BF16)

    kn = _dot(ckv, wk_ref[...])
    krr = _rope_tile(kr, cosv, sa, sb).astype(BF16)
    for hh in range(MLA_HEADS):
        lo = hh * HEAD_PAD
        k_ref[0, :, lo:lo + LANES] = kn[:, hh * LANES:(hh + 1) * LANES].astype(BF16)
        k_ref[0, :, lo + LANES:lo + HEAD_PAD] = krr
    v_ref[0] = _dot(ckv, wv_ref[...]).astype(BF16)


def _mla_front_call(x, pos3, mod, g, win, gq, gkv, wq, wk, wv, rc, tm):
    b, s, d = x.shape
    hq = MLA_HEADS * HEAD_PAD
    hv = MLA_HEADS * V_HEAD
    const = lambda shape: pl.BlockSpec(shape, lambda bi, si: (0,) * len(shape))
    return pl.pallas_call(
        functools.partial(_mla_front_kernel, d=d),
        grid=(b, s // tm),
        in_specs=[
            pl.BlockSpec((1, tm, d), lambda bi, si: (bi, si, 0)),
            pl.BlockSpec((1, tm, 1), lambda bi, si: (bi, si, 0)),
            _mod_spec(mod, d, lambda bi, si: bi),
            const(g.shape), const(win.shape), const(gq.shape), const(gkv.shape),
            const(wq.shape), const(wk.shape), const(wv.shape), const(rc.shape),
        ],
        out_specs=[
            pl.BlockSpec((1, tm, hq), lambda bi, si: (bi, si, 0)),
            pl.BlockSpec((1, tm, hq), lambda bi, si: (bi, si, 0)),
            pl.BlockSpec((1, tm, hv), lambda bi, si: (bi, si, 0)),
        ],
        out_shape=[
            jax.ShapeDtypeStruct((b, s, hq), BF16),
            jax.ShapeDtypeStruct((b, s, hq), BF16),
            jax.ShapeDtypeStruct((b, s, hv), BF16),
        ],
        compiler_params=_cparams(("arbitrary", "arbitrary")),
        name="mla_front",
    )(x, pos3, mod[0], g, win, gq, gkv, wq, wk, wv, rc)


def _attn_kernel(q_ref, k_ref, v_ref, o_ref, *, t, heads):
    qi = pl.program_id(2)
    tril = (lax.broadcasted_iota(jnp.int32, (t, t), 1)
            <= lax.broadcasted_iota(jnp.int32, (t, t), 0))

    def block(j, carry, masked):
        start = pl.multiple_of(j * t, t)
        out = []
        for hh in range(heads):
            m, l, acc = carry[hh]
            q = q_ref[0, :, hh * HEAD_PAD:(hh + 1) * HEAD_PAD]
            k = k_ref[0, pl.ds(start, t), hh * HEAD_PAD:(hh + 1) * HEAD_PAD]
            v = v_ref[0, pl.ds(start, t), hh * V_HEAD:(hh + 1) * V_HEAD]
            s = lax.dot_general(q, k, (((1,), (1,)), ((), ())), preferred_element_type=F32)
            if masked:
                s = jnp.where(tril, s, -jnp.inf)
            m_new = jnp.maximum(m, jnp.max(s, axis=-1, keepdims=True))
            alpha = jnp.exp(m - m_new)
            p = jnp.exp(s - m_new)
            l = alpha * l + jnp.sum(p, axis=-1, keepdims=True)
            acc = alpha * acc + _dot(p.astype(BF16), v)
            out.append((m_new, l, acc))
        return tuple(out)

    init = tuple((jnp.full((t, 1), -jnp.inf, F32), jnp.zeros((t, 1), F32),
                  jnp.zeros((t, V_HEAD), F32)) for _ in range(heads))
    carry = lax.fori_loop(0, qi, functools.partial(block, masked=False), init)
    carry = block(qi, carry, True)
    for hh in range(heads):
        _, l, acc = carry[hh]
        o_ref[0, :, hh * V_HEAD:(hh + 1) * V_HEAD] = (acc / l).astype(BF16)


def _attn_call(q, k, v, t, heads):
    b, s, _ = q.shape
    return pl.pallas_call(
        functools.partial(_attn_kernel, t=t, heads=heads),
        grid=(b, MLA_HEADS // heads, s // t),
        in_specs=[
            pl.BlockSpec((1, t, heads * HEAD_PAD), lambda bi, hi, qi: (bi, qi, hi)),
            pl.BlockSpec((1, s, heads * HEAD_PAD), lambda bi, hi, qi: (bi, 0, hi)),
            pl.BlockSpec((1, s, heads * V_HEAD), lambda bi, hi, qi: (bi, 0, hi)),
        ],
        out_specs=pl.BlockSpec((1, t, heads * V_HEAD), lambda bi, hi, qi: (bi, qi, hi)),
        out_shape=jax.ShapeDtypeStruct((b, s, MLA_HEADS * V_HEAD), BF16),
        compiler_params=_cparams(("arbitrary", "arbitrary", "arbitrary")),
        name="attention",
    )(q, k, v)


def _attn_out_kernel(x_ref, o_ref, w_ref, moda_ref, modb_ref, g_ref, x1_ref, h_ref, *, d):
    y = _dot(o_ref[0], w_ref[...])
    x1 = x_ref[0] + moda_ref[0][:, 2 * d:3 * d] * y
    x1_ref[0] = x1
    h_ref[0] = _adaln(x1, g_ref[...], modb_ref[0], d).astype(BF16)


def _attn_out_call(x, o, w, moda, modb, g, tm):
    b, s, d = x.shape
    tile = lambda: pl.BlockSpec((1, tm, d), lambda bi, si: (bi, si, 0))
    modspec = lambda mod: _mod_spec(mod, d, lambda bi, si: bi)
    return pl.pallas_call(
        functools.partial(_attn_out_kernel, d=d),
        grid=(b, s // tm),
        in_specs=[tile(), tile(), pl.BlockSpec(w.shape, lambda bi, si: (0, 0)),
                  modspec(moda), modspec(modb), pl.BlockSpec(g.shape, lambda bi, si: (0, 0))],
        out_specs=[tile(), tile()],
        out_shape=[jax.ShapeDtypeStruct((b, s, d), F32), jax.ShapeDtypeStruct((b, s, d), BF16)],
        compiler_params=_cparams(("arbitrary", "arbitrary")),
        name="attn_out",
    )(x, o, w, moda[0], modb[0], g)


def _ffn_body(valid, h_bf, wg_ref, wu_ref, wd_ref, o_ref, tm, sub, extra=None):
    n_sub = tm // sub
    for n in range(1, n_sub + 1):
        lo = (n - 1) * sub
        cond = valid > lo if n == n_sub else jnp.logical_and(valid > lo, valid <= n * sub)

        @pl.when(cond)
        def _():
            if extra is not None:
                extra()
            rows = slice(0, n * sub)
            h = h_bf[rows, :]
            g = _dot(h, wg_ref[0].astype(BF16))
            u = _dot(h, wu_ref[0].astype(BF16))
            a = (g * jax.nn.sigmoid(g) * u).astype(BF16)
            o_ref[rows, :] += _dot(a, wd_ref[0].astype(BF16))


def _ffn_dense_kernel(te_ref, tv_ref, h_ref, wg_ref, wu_ref, wd_ref, o_ref, *, tm, sub):
    i = pl.program_id(0)
    j = pl.program_id(1)

    @pl.when(j == 0)
    def _():
        o_ref[...] = jnp.zeros_like(o_ref)

    _ffn_body(tv_ref[i], h_ref, wg_ref, wu_ref, wd_ref, o_ref, tm, sub)


def _ffn_gather_kernel(te_ref, tv_ref, src_ref, srcn_ref, h_hbm, wg_ref, wu_ref, wd_ref, o_ref,
                       hf_ref, hb_ref, sem, *, tm, sub, rows_per_step):
    i = pl.program_id(0)
    j = pl.program_id(1)
    valid = tv_ref[i]
    n_rows = hf_ref.shape[0]

    def row_copy(idx_ref, r):
        return pltpu.make_async_copy(h_hbm.at[pl.ds(idx_ref[0, 0, r], 1), :],
                                     hf_ref.at[pl.ds(r, 1), :], sem)

    def wait_rows():
        pltpu.make_async_copy(h_hbm.at[pl.ds(0, n_rows), :], hf_ref, sem).wait()

    @pl.when(j == 0)
    def _():
        o_ref[...] = jnp.zeros_like(o_ref)

        @pl.when(jnp.logical_and(i == 0, valid > 0))
        def _():
            def issue(r, c):
                row_copy(src_ref, r).start()
                return c
            lax.fori_loop(0, n_rows, issue, 0)
            wait_rows()

        @pl.when(jnp.logical_and(i > 0, tv_ref[jnp.maximum(i - 1, 0)] > 0))
        def _():
            wait_rows()

        @pl.when(valid > 0)
        def _():
            hb_ref[...] = hf_ref[0:tm, :].astype(BF16)

    def request_next_rows():
        for q in range(rows_per_step):
            row_copy(srcn_ref, j * rows_per_step + q).start()

    _ffn_body(valid, hb_ref, wg_ref, wu_ref, wd_ref, o_ref, tm, sub, request_next_rows)


def _ffn_call(h, wg, wu, wd, tile_expert, tile_valid, src, tm, sub, tf):
    d = wg.shape[1]
    f = wg.shape[2]
    nf = f // tf
    nt = tile_expert.shape[0]

    def jj(i, j, tv):
        return jnp.where(tv[i] > 0, j, nf - 1)

    w_specs = [
        pl.BlockSpec((1, d, tf), lambda i, j, te, tv: (te[i], 0, jj(i, j, tv))),
        pl.BlockSpec((1, d, tf), lambda i, j, te, tv: (te[i], 0, jj(i, j, tv))),
        pl.BlockSpec((1, tf, d), lambda i, j, te, tv: (te[i], jj(i, j, tv), 0)),
    ]
    out_spec = pl.BlockSpec((tm, d), lambda i, j, te, tv: (i, 0))
    out_shape = jax.ShapeDtypeStruct((nt * tm, d), F32)
    if src is None:
        grid_spec = pltpu.PrefetchScalarGridSpec(
            num_scalar_prefetch=2, grid=(nt, nf),
            in_specs=[pl.BlockSpec((tm, d), lambda i, j, te, tv: (i, 0))] + w_specs,
            out_specs=out_spec)
        return pl.pallas_call(
            functools.partial(_ffn_dense_kernel, tm=tm, sub=sub), grid_spec=grid_spec,
            out_shape=out_shape,
            compiler_params=_cparams(("arbitrary", "arbitrary")), name="ffn_dense",
        )(tile_expert, tile_valid, h, wg, wu, wd)
    n_rows = src.shape[2]
    assert n_rows % nf == 0 and n_rows >= tm
    grid_spec = pltpu.PrefetchScalarGridSpec(
        num_scalar_prefetch=2, grid=(nt, nf),
        in_specs=[pl.BlockSpec((1, 1, n_rows), lambda i, j, te, tv: (i, 0, 0),
                               memory_space=pltpu.SMEM),
                  pl.BlockSpec((1, 1, n_rows),
                               lambda i, j, te, tv: (jnp.minimum(i + 1, nt - 1), 0, 0),
                               memory_space=pltpu.SMEM),
                  pl.BlockSpec(memory_space=pl.ANY)] + w_specs,
        out_specs=out_spec,
        scratch_shapes=[pltpu.VMEM((n_rows, d), F32), pltpu.VMEM((tm, d), BF16),
                        pltpu.SemaphoreType.DMA(())])
    return pl.pallas_call(
        functools.partial(_ffn_gather_kernel, tm=tm, sub=sub, rows_per_step=n_rows // nf),
        grid_spec=grid_spec, out_shape=out_shape,
        compiler_params=_cparams(("arbitrary", "arbitrary")), name="ffn_moe",
    )(tile_expert, tile_valid, src, src, h, wg, wu, wd)


def _conv_front_kernel(x1_ref, y_ref, moda_ref, modb_ref, g_ref, w_ref, b_ref,
                       x2_ref, glu_ref, *, d):
    x2 = x1_ref[0] + moda_ref[0][:, 2 * d:3 * d] * y_ref[0]
    x2_ref[0] = x2
    h = _adaln(x2, g_ref[...], modb_ref[0], d).astype(BF16)
    u = _dot(h, w_ref[...]) + b_ref[...]
    glu_ref[0] = u[:, 0:d] * jax.nn.sigmoid(u[:, d:2 * d])


def _conv_front_call(x1, y, moda, modb, g, w, bias, tm):
    b, s, d = x1.shape
    tile = lambda: pl.BlockSpec((1, tm, d), lambda bi, si: (bi, si, 0))
    modspec = lambda mod: _mod_spec(mod, d, lambda bi, si: bi)
    const = lambda a: pl.BlockSpec(a.shape, lambda bi, si: (0,) * a.ndim)
    return pl.pallas_call(
        functools.partial(_conv_front_kernel, d=d),
        grid=(b, s // tm),
        in_specs=[tile(), tile(), modspec(moda), modspec(modb), const(g), const(w),
                  const(bias)],
        out_specs=[tile(), tile()],
        out_shape=[jax.ShapeDtypeStruct((b, s, d), F32), jax.ShapeDtypeStruct((b, s, d), F32)],
        compiler_params=_cparams(("arbitrary", "arbitrary")),
        name="conv_front",
    )(x1, y, moda[0], modb[0], g, w, bias)


def _conv_back_kernel(glu_ref, halo_ref, x2_ref, moda_ref, modb_ref, wdw_ref, bdw_ref,
                      lng_ref, lnb_ref, w2_ref, b2_ref, g_ref, wrh_ref, wrl_ref,
                      x3_ref, h4_ref, route_ref, ext_ref, conv_ref, base_ref, sh_ref,
                      *, d, tm, rows, cols):
    bi = pl.program_id(0)
    si = pl.program_id(1)

    @pl.when(jnp.logical_and(bi == 0, si == 0))
    def _():
        base_ref[...] = jnp.zeros_like(base_ref)

    ext_ref[0:HALO, :] = jnp.where(si == 0, 0.0, halo_ref[0])
    ext_ref[HALO:HALO + tm, :] = glu_ref[0]

    off = HALO - (CONV_WIDTH - 1)
    span = sh_ref.shape[1]
    for c0 in range(0, d, cols):
        for r in range(1, SUBLANES):
            sh_ref[r - 1] = ext_ref[r:r + span, c0:c0 + cols]
        for r0 in range(0, tm, rows):
            groups = (rows // SUBLANES, SUBLANES, cols)
            acc = jnp.zeros(groups, F32) + bdw_ref[:, c0:c0 + cols][None]
            for k in range(CONV_WIDTH):
                r = (off + k) % SUBLANES
                a = r0 + off + k - r
                if r == 0:
                    tap = ext_ref[a:a + rows, c0:c0 + cols]
                else:
                    tap = sh_ref[r - 1, a:a + rows, :]
                acc = acc + wdw_ref[k, :, c0:c0 + cols][None] * tap.reshape(groups)
            conv_ref[r0:r0 + rows, c0:c0 + cols] = acc.reshape(rows, cols)

    u = conv_ref[...]
    mu = jnp.mean(u, axis=-1, keepdims=True)
    uc = u - mu
    var = jnp.mean(uc * uc, axis=-1, keepdims=True)
    u = uc * lax.rsqrt(var + EPS) * lng_ref[...] + lnb_ref[...]
    u = (u * jax.nn.sigmoid(u)).astype(BF16)
    y = _dot(u, w2_ref[...]) + b2_ref[...]
    x3 = x2_ref[0] + moda_ref[0][:, 2 * d:3 * d] * y
    x3_ref[0] = x3
    h4 = _adaln(x3, g_ref[...], modb_ref[0], d)
    h4_ref[0] = h4

    hh = h4.astype(BF16)
    hl = (h4 - hh.astype(F32)).astype(BF16)
    logits = _dot(hh, wrh_ref[...]) + _dot(hl, wrh_ref[...]) + _dot(hh, wrl_ref[...])
    lane = lax.broadcasted_iota(jnp.int32, (tm, LANES), 1).astype(F32)
    lg = jnp.where(lane < N_EXPERTS, logits, -jnp.inf)
    m1 = jnp.max(lg, axis=-1, keepdims=True)
    i1 = jnp.min(jnp.where(lg == m1, lane, float(LANES)), axis=-1, keepdims=True)
    lg2 = jnp.where(lane == i1, -jnp.inf, lg)
    m2 = jnp.max(lg2, axis=-1, keepdims=True)
    i2 = jnp.min(jnp.where(lg2 == m2, lane, float(LANES)), axis=-1, keepdims=True)
    e2 = jnp.exp(m2 - m1)
    w0 = 1.0 / (1.0 + e2)
    w1 = e2 / (1.0 + e2)
    oh0 = (lane == i1).astype(F32)
    oh1 = (lane == i2).astype(F32)
    both = oh0 + oh1
    tri = (lax.broadcasted_iota(jnp.int32, (tm, tm), 0)
           > lax.broadcasted_iota(jnp.int32, (tm, tm), 1)).astype(BF16)
    cnt = _dot(tri, both.astype(BF16)) + base_ref[...]
    r0 = jnp.sum(cnt * oh0, axis=-1, keepdims=True)
    r1 = jnp.sum(cnt * oh1, axis=-1, keepdims=True)
    base_ref[...] = base_ref[...] + jnp.sum(both, axis=0, keepdims=True)
    route = jnp.where(lane == 0.0, i1, 0.0)
    route = jnp.where(lane == 1.0, i2, route)
    route = jnp.where(lane == 2.0, w0, route)
    route = jnp.where(lane == 3.0, w1, route)
    route = jnp.where(lane == 4.0, r0, route)
    route = jnp.where(lane == 5.0, r1, route)
    route_ref[0] = route


def _conv_back_call(glu, x2, moda, modb, wdw, bdw, lng, lnb, w2, b2, g, wrh, wrl, tm):
    b, s, d = x2.shape
    tile = lambda: pl.BlockSpec((1, tm, d), lambda bi, si: (bi, si, 0))
    modspec = lambda mod: _mod_spec(mod, d, lambda bi, si: bi)
    const = lambda a: pl.BlockSpec(a.shape, lambda bi, si: (0,) * a.ndim)
    hpt = tm // HALO
    halo = pl.BlockSpec((1, HALO, d), lambda bi, si: (bi, jnp.maximum(si * hpt - 1, 0), 0))
    cols = 2 * LANES
    return pl.pallas_call(
        functools.partial(_conv_back_kernel, d=d, tm=tm, rows=64, cols=cols),
        grid=(b, s // tm),
        in_specs=[tile(), halo, tile(), modspec(moda), modspec(modb), const(wdw), const(bdw),
                  const(lng), const(lnb), const(w2), const(b2), const(g), const(wrh), const(wrl)],
        out_specs=[tile(), tile(), pl.BlockSpec((1, tm, LANES), lambda bi, si: (bi, si, 0))],
        out_shape=[jax.ShapeDtypeStruct((b, s, d), F32), jax.ShapeDtypeStruct((b, s, d), F32),
                   jax.ShapeDtypeStruct((b, s, LANES), F32)],
        scratch_shapes=[pltpu.VMEM((tm + HALO, d), F32), pltpu.VMEM((tm, d), F32),
                        pltpu.VMEM((1, LANES), F32),
                        pltpu.VMEM((SUBLANES - 1, tm + HALO - SUBLANES, cols), F32)],
        compiler_params=_cparams(("arbitrary", "arbitrary")),
        name="conv_back",
    )(glu, glu, x2, moda[0], modb[0], wdw, bdw, lng, lnb, w2, b2, g, wrh, wrl)


def _combine_kernel(p0_ref, p1_ref, p0n_ref, p1n_ref, x_ref, route_ref, mod_ref, fg_ref, ys_hbm,
                    o_ref, y_ref, sem, *, d, tm):
    i = pl.program_id(0)
    slot = i % 2

    def request(pa_ref, pb_ref, sl):
        def issue(r, c):
            pltpu.make_async_copy(ys_hbm.at[pl.ds(pa_ref[0, 0, r], 1), :],
                                  y_ref.at[sl, 0, pl.ds(r, 1), :], sem.at[sl, 0]).start()
            pltpu.make_async_copy(ys_hbm.at[pl.ds(pb_ref[0, 0, r], 1), :],
                                  y_ref.at[sl, 1, pl.ds(r, 1), :], sem.at[sl, 1]).start()
            return c
        lax.fori_loop(0, tm, issue, 0, unroll=8)

    @pl.when(i == 0)
    def _():
        request(p0_ref, p1_ref, 0)

    @pl.when(i + 1 < pl.num_programs(0))
    def _():
        request(p0n_ref, p1n_ref, 1 - slot)

    for pick in range(2):
        pltpu.make_async_copy(ys_hbm.at[pl.ds(0, tm), :], y_ref.at[slot, pick],
                              sem.at[slot, pick]).wait()
    w0 = route_ref[:, 2:3]
    w1 = route_ref[:, 3:4]
    y = w0 * y_ref[slot, 0] + w1 * y_ref[slot, 1]
    x = x_ref[...] + mod_ref[0][:, 2 * d:3 * d] * y
    o_ref[...] = _rms(x, fg_ref[...])


def _combine_call(p0, p1, x3, route, mod, fg, ys, tm, tiles_per_batch):
    t, d = x3.shape
    nt = t // tm
    nxt = lambda i: (jnp.minimum(i + 1, nt - 1), 0, 0)
    return pl.pallas_call(
        functools.partial(_combine_kernel, d=d, tm=tm),
        grid=(nt,),
        in_specs=[
            pl.BlockSpec((1, 1, tm), lambda i: (i, 0, 0), memory_space=pltpu.SMEM),
            pl.BlockSpec((1, 1, tm), lambda i: (i, 0, 0), memory_space=pltpu.SMEM),
            pl.BlockSpec((1, 1, tm), nxt, memory_space=pltpu.SMEM),
            pl.BlockSpec((1, 1, tm), nxt, memory_space=pltpu.SMEM),
            pl.BlockSpec((tm, d), lambda i: (i, 0)),
            pl.BlockSpec((tm, LANES), lambda i: (i, 0)),
            _mod_spec(mod, d, lambda i: i // tiles_per_batch),
            pl.BlockSpec(fg.shape, lambda i: (0, 0)),
            pl.BlockSpec(memory_space=pl.ANY),
        ],
        out_specs=pl.BlockSpec((tm, d), lambda i: (i, 0)),
        out_shape=jax.ShapeDtypeStruct((t, d), F32),
        scratch_shapes=[pltpu.VMEM((2, 2, tm, d), F32), pltpu.SemaphoreType.DMA((2, 2))],
        compiler_params=_cparams(("arbitrary",)),
        name="combine",
    )(p0, p1, p0, p1, x3, route, mod[0], fg, ys)


def _rope_consts():
    lane = np.arange(LANES)
    half = QK_ROPE // 2
    freq = np.where(lane < QK_ROPE, ROPE_THETA ** (-(2.0 * (lane % half)) / QK_ROPE), 0.0)
    sa = np.where((lane >= half) & (lane < QK_ROPE), 1.0, 0.0)
    sb = np.where(lane < half, -1.0, 0.0)
    rc = np.zeros((8, LANES), np.float32)
    rc[0], rc[1], rc[2] = freq, sa, sb
    return jnp.asarray(rc)


def _moe_plan(route, tm, n_tiles, n_rows):
    t = route.shape[0]
    assert n_tiles == (2 * t) // tm + N_EXPERTS and (2 * t) % tm == 0
    e0 = route[:, 0].astype(jnp.int32)
    e1 = route[:, 1].astype(jnp.int32)
    r0 = route[:, 4].astype(jnp.int32)
    r1 = route[:, 5].astype(jnp.int32)
    ids = jnp.arange(N_EXPERTS, dtype=jnp.int32)
    counts = (jnp.sum(e0[:, None] == ids[None, :], axis=0)
              + jnp.sum(e1[:, None] == ids[None, :], axis=0)).astype(jnp.int32)
    tiles = (counts + tm - 1) // tm
    tile_end = jnp.cumsum(tiles)
    tile_start = tile_end - tiles
    pos0 = tile_start[e0] * tm + r0
    pos1 = tile_start[e1] * tm + r1
    tok = jnp.arange(t, dtype=jnp.int32)
    pos = jnp.concatenate([pos0, pos1])
    src = jnp.zeros((n_tiles * n_rows,), jnp.int32).at[(pos // tm) * n_rows + pos % tm].set(
        jnp.concatenate([tok, tok]), unique_indices=True, mode="promise_in_bounds")
    ti = jnp.arange(n_tiles, dtype=jnp.int32)
    used = tile_end[-1]
    te = jnp.sum(tile_end[None, :] <= jnp.minimum(ti, used - 1)[:, None], axis=1)
    te = jnp.minimum(te, N_EXPERTS - 1).astype(jnp.int32)
    tv = jnp.clip(counts[te] - (ti - tile_start[te]) * tm, 0, tm)
    tv = jnp.where(ti < used, tv, 0).astype(jnp.int32)
    return pos0, pos1, src.reshape(n_tiles, 1, n_rows), te, tv


def kernel(x, c, positions, ada_w, ada_b, norm_g, mla_w_in, mla_g_q, mla_g_kv, mla_w_uq,
           mla_w_ukv, mla_w_out, conv_w_pw1, conv_b_pw1, conv_w_dw, conv_b_dw, conv_ln_g,
           conv_ln_b, conv_w_pw2, conv_b_pw2, ffn_w_gate, ffn_w_up, ffn_w_down, moe_w_router,
           moe_w_gate, moe_w_up, moe_w_down, final_g):
    b, s, d = x.shape
    t = b * s
    f = ffn_w_gate.shape[-1]
    assert ada_w.shape[0] == 2 and mla_w_in.shape[0] == 1 and conv_w_pw1.shape[0] == 1
    assert moe_w_gate.shape[1] == N_EXPERTS and conv_w_dw.shape[1] == CONV_WIDTH

    scale = (QK_NOPE + QK_ROPE) ** -0.5
    w_in = jnp.pad(mla_w_in[0], ((0, 0), (0, LANES - QK_ROPE))).astype(BF16)
    wq = mla_w_uq[0].reshape(Q_LORA, MLA_HEADS, QK_NOPE + QK_ROPE) * scale
    wq = jnp.pad(wq, ((0, 0), (0, 0), (0, HEAD_PAD - QK_NOPE - QK_ROPE)))
    wq = wq.reshape(Q_LORA, MLA_HEADS * HEAD_PAD).astype(BF16)
    wkv = mla_w_ukv[0].reshape(KV_LORA, MLA_HEADS, QK_NOPE + V_HEAD)
    wk = wkv[:, :, :QK_NOPE].reshape(KV_LORA, MLA_HEADS * QK_NOPE).astype(BF16)
    wv = wkv[:, :, QK_NOPE:].reshape(KV_LORA, MLA_HEADS * V_HEAD).astype(BF16)
    w_out = mla_w_out[0].astype(BF16)
    w_pw1 = conv_w_pw1[0].astype(BF16)
    w_pw2 = conv_w_pw2[0].astype(BF16)
    wr = jnp.pad(moe_w_router[0], ((0, 0), (0, LANES - N_EXPERTS)))
    wr_hi = wr.astype(BF16)
    wr_lo = (wr - wr_hi.astype(F32)).astype(BF16)
    row = lambda v: v.reshape(1, -1)

    mods = _ada_call(c, ada_w.reshape(4, d, 3 * d), ada_b.reshape(4, 3 * d))
    mods = mods.reshape(4, b, 1, 3 * d)
    mod = [(mods, i) for i in range(4)]

    q, k, v = _mla_front_call(x, positions.reshape(b, s, 1), mod[0], row(norm_g[0, 0]), w_in,
                              row(mla_g_q[0]), row(mla_g_kv[0]), wq, wk, wv, _rope_consts(),
                              tm=256)
    o = _attn_call(q, k, v, t=512, heads=4)
    x1, h2 = _attn_out_call(x, o, w_out, mod[0], mod[1], row(norm_g[0, 1]), tm=512)
    tm_d = 1024
    nt_d = t // tm_d
    y1 = _ffn_call(h2.reshape(t, d), ffn_w_gate.astype(BF16), ffn_w_up.astype(BF16),
                   ffn_w_down.astype(BF16), jnp.zeros((nt_d,), jnp.int32),
                   jnp.full((nt_d,), tm_d, jnp.int32), None, tm=tm_d, sub=tm_d, tf=512)

    x2, glu = _conv_front_call(x1, y1.reshape(b, s, d), mod[1], mod[2], row(norm_g[1, 0]),
                               w_pw1, row(conv_b_pw1[0]), tm=256)
    w_dw8 = jnp.broadcast_to(conv_w_dw[0][:, None, :], (CONV_WIDTH, SUBLANES, d))
    b_dw8 = jnp.broadcast_to(row(conv_b_dw[0]), (SUBLANES, d))
    x3, h4, route = _conv_back_call(glu, x2, mod[2], mod[3], w_dw8, b_dw8,
                                    row(conv_ln_g[0]), row(conv_ln_b[0]), w_pw2,
                                    row(conv_b_pw2[0]), row(norm_g[1, 1]), wr_hi, wr_lo, tm=256)
    route = route.reshape(t, LANES)
    tm_e = 1024
    tf_e = 256
    nt_e = (2 * t) // tm_e + N_EXPERTS
    nf_e = f // tf_e
    rps = -(-tm_e // nf_e)
    while (rps * nf_e) % SUBLANES:
        rps += 1
    rows_e = rps * nf_e
    pos0, pos1, src, te, tv = _moe_plan(route, tm_e, nt_e, rows_e)
    ys = _ffn_call(h4.reshape(t, d), moe_w_gate[0], moe_w_up[0], moe_w_down[0], te, tv, src,
                   tm=tm_e, sub=256, tf=tf_e)
    tm_c = 256
    out = _combine_call(pos0.reshape(t // tm_c, 1, tm_c), pos1.reshape(t // tm_c, 1, tm_c),
                        x3.reshape(t, d), route, mod[3], row(final_g), ys, tm=tm_c,
                        tiles_per_batch=s // tm_c)
    return out.reshape(b, s, d)
```

```python
import functools

import numpy as np
import jax
import jax.numpy as jnp
from jax import lax
from jax.experimental import pallas as pl
from jax.experimental.pallas import tpu as pltpu

F32 = jnp.float32
BF16 = jnp.bfloat16

MLA_HEADS = 16
Q_LORA = 512
KV_LORA = 512
QK_NOPE = 128
QK_ROPE = 64
V_HEAD = 128
ROPE_THETA = 10000.0
CONV_WIDTH = 31
N_EXPERTS = 8
EPS = 1e-6

LANES = 128
SUBLANES = 8
HEAD_PAD = 2 * LANES
HALO = 32
VMEM_LIMIT = 56 * 1024 * 1024


def _cparams(sem):
    return pltpu.CompilerParams(dimension_semantics=sem, vmem_limit_bytes=VMEM_LIMIT)


def _rms(x, g):
    return x * lax.rsqrt(jnp.mean(x * x, axis=-1, keepdims=True) + EPS) * g


def _adaln(x, g, mod, d):
    return _rms(x, g) * (1.0 + mod[:, d:2 * d]) + mod[:, 0:d]


def _dot(a, b):
    return jnp.dot(a, b, preferred_element_type=F32)


def _mod_spec(mod, d, batch_of):
    k = mod[1]
    return pl.BlockSpec((None, 1, 1, 3 * d), lambda *g: (k, batch_of(*g), 0, 0))


def _ada_kernel(c_ref, w_ref, b_ref, o_ref):
    c = c_ref[...]
    sc = (c * jax.nn.sigmoid(c)).astype(BF16)
    o_ref[0] = _dot(sc, w_ref[0].astype(BF16)) + b_ref[0]


def _ada_call(c, ada_w, ada_b):
    n, d, d3 = ada_w.shape
    b = c.shape[0]
    tn = 1024
    return pl.pallas_call(
        _ada_kernel,
        grid=(n, d3 // tn),
        in_specs=[
            pl.BlockSpec((b, d), lambda i, j: (0, 0)),
            pl.BlockSpec((1, d, tn), lambda i, j: (i, 0, j)),
            pl.BlockSpec((1, 1, tn), lambda i, j: (i, 0, j)),
        ],
        out_specs=pl.BlockSpec((1, b, tn), lambda i, j: (i, 0, j)),
        out_shape=jax.ShapeDtypeStruct((n, b, d3), F32),
        compiler_params=_cparams(("arbitrary", "arbitrary")),
        name="ada",
    )(c, ada_w, ada_b.reshape(n, 1, d3))


def _rope_tile(t, cosv, sa, sb):
    return t * cosv + pltpu.roll(t, 32, 1) * sa + pltpu.roll(t, 96, 1) * sb


def _mla_front_kernel(x_ref, pos_ref, mod_ref, g_ref, win_ref, gq_ref, gkv_ref,
                      wq_ref, wk_ref, wv_ref, rc_ref, q_ref, k_ref, v_ref, *, d):
    x = x_ref[0]
    h = _adaln(x, g_ref[...], mod_ref[0], d).astype(BF16)
    lat = _dot(h, win_ref[...])
    cq = _rms(lat[:, 0:Q_LORA], gq_ref[...]).astype(BF16)
    ckv = _rms(lat[:, Q_LORA:Q_LORA + KV_LORA], gkv_ref[...]).astype(BF16)
    kr = lat[:, Q_LORA + KV_LORA:Q_LORA + KV_LORA + LANES]

    ang = pos_ref[0].astype(F32) * rc_ref[0:1, :]
    cosv = jnp.cos(ang)
    sinv = jnp.sin(ang)
    sa = sinv * rc_ref[1:2, :]
    sb = sinv * rc_ref[2:3, :]

    q = _dot(cq, wq_ref[...])
    for hh in range(MLA_HEADS):
        lo = hh * HEAD_PAD
        q_ref[0, :, lo:lo + LANES] = q[:, lo:lo + LANES].astype(BF16)
        q_ref[0, :, lo + LANES:lo + HEAD_PAD] = _rope_tile(
            q[:, lo + LANES:lo + HEAD_PAD], cosv, sa, sb).astype(BF16)

    kn = _dot(ckv, wk_ref[...])
    krr = _rope_tile(kr, cosv, sa, sb).astype(BF16)
    for hh in range(MLA_HEADS):
        lo = hh * HEAD_PAD
        k_ref[0, :, lo:lo + LANES] = kn[:, hh * LANES:(hh + 1) * LANES].astype(BF16)
        k_ref[0, :, lo + LANES:lo + HEAD_PAD] = krr
    v_ref[0] = _dot(ckv, wv_ref[...]).astype(BF16)


def _mla_front_call(x, pos3, mod, g, win, gq, gkv, wq, wk, wv, rc, tm):
    b, s, d = x.shape
    hq = MLA_HEADS * HEAD_PAD
    hv = MLA_HEADS * V_HEAD
    const = lambda shape: pl.BlockSpec(shape, lambda bi, si: (0,) * len(shape))
    return pl.pallas_call(
        functools.partial(_mla_front_kernel, d=d),
        grid=(b, s // tm),
        in_specs=[
            pl.BlockSpec((1, tm, d), lambda bi, si: (bi, si, 0)),
            pl.BlockSpec((1, tm, 1), lambda bi, si: (bi, si, 0)),
            _mod_spec(mod, d, lambda bi, si: bi),
            const(g.shape), const(win.shape), const(gq.shape), const(gkv.shape),
            const(wq.shape), const(wk.shape), const(wv.shape), const(rc.shape),
        ],
        out_specs=[
            pl.BlockSpec((1, tm, hq), lambda bi, si: (bi, si, 0)),
            pl.BlockSpec((1, tm, hq), lambda bi, si: (bi, si, 0)),
            pl.BlockSpec((1, tm, hv), lambda bi, si: (bi, si, 0)),
        ],
        out_shape=[
            jax.ShapeDtypeStruct((b, s, hq), BF16),
            jax.ShapeDtypeStruct((b, s, hq), BF16),
            jax.ShapeDtypeStruct((b, s, hv), BF16),
        ],
        compiler_params=_cparams(("arbitrary", "arbitrary")),
        name="mla_front",
    )(x, pos3, mod[0], g, win, gq, gkv, wq, wk, wv, rc)


def _attn_kernel(q_ref, k_ref, v_ref, o_ref, *, t, heads):
    qi = pl.program_id(2)
    tril = (lax.broadcasted_iota(jnp.int32, (t, t), 1)
            <= lax.broadcasted_iota(jnp.int32, (t, t), 0))

    def block(j, carry, masked):
        start = pl.multiple_of(j * t, t)
        out = []
        for hh in range(heads):
            m, l, acc = carry[hh]
            q = q_ref[0, :, hh * HEAD_PAD:(hh + 1) * HEAD_PAD]
            k = k_ref[0, pl.ds(start, t), hh * HEAD_PAD:(hh + 1) * HEAD_PAD]
            v = v_ref[0, pl.ds(start, t), hh * V_HEAD:(hh + 1) * V_HEAD]
            s = lax.dot_general(q, k, (((1,), (1,)), ((), ())), preferred_element_type=F32)
            if masked:
                s = jnp.where(tril, s, -jnp.inf)
            m_new = jnp.maximum(m, jnp.max(s, axis=-1, keepdims=True))
            alpha = jnp.exp(m - m_new)
            p = jnp.exp(s - m_new)
            l = alpha * l + jnp.sum(p, axis=-1, keepdims=True)
            acc = alpha * acc + _dot(p.astype(BF16), v)
            out.append((m_new, l, acc))
        return tuple(out)

    init = tuple((jnp.full((t, 1), -jnp.inf, F32), jnp.zeros((t, 1), F32),
                  jnp.zeros((t, V_HEAD), F32)) for _ in range(heads))
    carry = lax.fori_loop(0, qi, functools.partial(block, masked=False), init)
    carry = block(qi, carry, True)
    for hh in range(heads):
        _, l, acc = carry[hh]
        o_ref[0, :, hh * V_HEAD:(hh + 1) * V_HEAD] = (acc / l).astype(BF16)


def _attn_call(q, k, v, t, heads):
    b, s, _ = q.shape
    return pl.pallas_call(
        functools.partial(_attn_kernel, t=t, heads=heads),
        grid=(b, MLA_HEADS // heads, s // t),
        in_specs=[
            pl.BlockSpec((1, t, heads * HEAD_PAD), lambda bi, hi, qi: (bi, qi, hi)),
            pl.BlockSpec((1, s, heads * HEAD_PAD), lambda bi, hi, qi: (bi, 0, hi)),
            pl.BlockSpec((1, s, heads * V_HEAD), lambda bi, hi, qi: (bi, 0, hi)),
        ],
        out_specs=pl.BlockSpec((1, t, heads * V_HEAD), lambda bi, hi, qi: (bi, qi, hi)),
        out_shape=jax.ShapeDtypeStruct((b, s, MLA_HEADS * V_HEAD), BF16),
        compiler_params=_cparams(("arbitrary", "arbitrary", "arbitrary")),
        name="attention",
    )(q, k, v)


def _attn_out_kernel(x_ref, o_ref, w_ref, moda_ref, modb_ref, g_ref, x1_ref, h_ref, *, d):
    y = _dot(o_ref[0], w_ref[...])
    x1 = x_ref[0] + moda_ref[0][:, 2 * d:3 * d] * y
    x1_ref[0] = x1
    h_ref[0] = _adaln(x1, g_ref[...], modb_ref[0], d).astype(BF16)


def _attn_out_call(x, o, w, moda, modb, g, tm):
    b, s, d = x.shape
    tile = lambda: pl.BlockSpec((1, tm, d), lambda bi, si: (bi, si, 0))
    modspec = lambda mod: _mod_spec(mod, d, lambda bi, si: bi)
    return pl.pallas_call(
        functools.partial(_attn_out_kernel, d=d),
        grid=(b, s // tm),
        in_specs=[tile(), tile(), pl.BlockSpec(w.shape, lambda bi, si: (0, 0)),
                  modspec(moda), modspec(modb), pl.BlockSpec(g.shape, lambda bi, si: (0, 0))],
        out_specs=[tile(), tile()],
        out_shape=[jax.ShapeDtypeStruct((b, s, d), F32), jax.ShapeDtypeStruct((b, s, d), BF16)],
        compiler_params=_cparams(("arbitrary", "arbitrary")),
        name="attn_out",
    )(x, o, w, moda[0], modb[0], g)


def _ffn_body(valid, h_bf, wg_ref, wu_ref, wd_ref, o_ref, tm, sub, extra=None):
    n_sub = tm // sub
    for n in range(1, n_sub + 1):
        lo = (n - 1) * sub
        cond = valid > lo if n == n_sub else jnp.logical_and(valid > lo, valid <= n * sub)

        @pl.when(cond)
        def _():
            if extra is not None:
                extra()
            rows = slice(0, n * sub)
            h = h_bf[rows, :]
            g = _dot(h, wg_ref[0].astype(BF16))
            u = _dot(h, wu_ref[0].astype(BF16))
            a = (g * jax.nn.sigmoid(g) * u).astype(BF16)
            o_ref[rows, :] += _dot(a, wd_ref[0].astype(BF16))


def _ffn_dense_kernel(te_ref, tv_ref, h_ref, wg_ref, wu_ref, wd_ref, o_ref, *, tm, sub):
    i = pl.program_id(0)
    j = pl.program_id(1)

    @pl.when(j == 0)
    def _():
        o_ref[...] = jnp.zeros_like(o_ref)

    _ffn_body(tv_ref[i], h_ref, wg_ref, wu_ref, wd_ref, o_ref, tm, sub)


def _ffn_gather_kernel(te_ref, tv_ref, src_ref, srcn_ref, h_hbm, wg_ref, wu_ref, wd_ref, o_ref,
                       hf_ref, hb_ref, sem, *, tm, sub, rows_per_step):
    i = pl.program_id(0)
    j = pl.program_id(1)
    valid = tv_ref[i]
    n_rows = hf_ref.shape[0]

    def row_copy(idx_ref, r):
        return pltpu.make_async_copy(h_hbm.at[pl.ds(idx_ref[0, 0, r], 1), :],
                                     hf_ref.at[pl.ds(r, 1), :], sem)

    def wait_rows():
        pltpu.make_async_copy(h_hbm.at[pl.ds(0, n_rows), :], hf_ref, sem).wait()

    @pl.when(j == 0)
    def _():
        o_ref[...] = jnp.zeros_like(o_ref)

        @pl.when(jnp.logical_and(i == 0, valid > 0))
        def _():
            def issue(r, c):
                row_copy(src_ref, r).start()
                return c
            lax.fori_loop(0, n_rows, issue, 0)
            wait_rows()

        @pl.when(jnp.logical_and(i > 0, tv_ref[jnp.maximum(i - 1, 0)] > 0))
        def _():
            wait_rows()

        @pl.when(valid > 0)
        def _():
            hb_ref[...] = hf_ref[0:tm, :].astype(BF16)

    def request_next_rows():
        for q in range(rows_per_step):
            row_copy(srcn_ref, j * rows_per_step + q).start()

    _ffn_body(valid, hb_ref, wg_ref, wu_ref, wd_ref, o_ref, tm, sub, request_next_rows)


def _ffn_call(h, wg, wu, wd, tile_expert, tile_valid, src, tm, sub, tf):
    d = wg.shape[1]
    f = wg.shape[2]
    nf = f // tf
    nt = tile_expert.shape[0]

    def jj(i, j, tv):
        return jnp.where(tv[i] > 0, j, nf - 1)

    w_specs = [
        pl.BlockSpec((1, d, tf), lambda i, j, te, tv: (te[i], 0, jj(i, j, tv))),
        pl.BlockSpec((1, d, tf), lambda i, j, te, tv: (te[i], 0, jj(i, j, tv))),
        pl.BlockSpec((1, tf, d), lambda i, j, te, tv: (te[i], jj(i, j, tv), 0)),
    ]
    out_spec = pl.BlockSpec((tm, d), lambda i, j, te, tv: (i, 0),
                            pipeline_mode=pl.Buffered(1) if src is not None else None)
    out_shape = jax.ShapeDtypeStruct((nt * tm, d), F32)
    if src is None:
        grid_spec = pltpu.PrefetchScalarGridSpec(
            num_scalar_prefetch=2, grid=(nt, nf),
            in_specs=[pl.BlockSpec((tm, d), lambda i, j, te, tv: (i, 0))] + w_specs,
            out_specs=out_spec)
        return pl.pallas_call(
            functools.partial(_ffn_dense_kernel, tm=tm, sub=sub), grid_spec=grid_spec,
            out_shape=out_shape,
            compiler_params=_cparams(("arbitrary", "arbitrary")), name="ffn_dense",
        )(tile_expert, tile_valid, h, wg, wu, wd)
    n_rows = src.shape[2]
    assert n_rows % nf == 0 and n_rows >= tm
    grid_spec = pltpu.PrefetchScalarGridSpec(
        num_scalar_prefetch=2, grid=(nt, nf),
        in_specs=[pl.BlockSpec((1, 1, n_rows), lambda i, j, te, tv: (i, 0, 0),
                               memory_space=pltpu.SMEM),
                  pl.BlockSpec((1, 1, n_rows),
                               lambda i, j, te, tv: (jnp.minimum(i + 1, nt - 1), 0, 0),
                               memory_space=pltpu.SMEM),
                  pl.BlockSpec(memory_space=pl.ANY)] + w_specs,
        out_specs=out_spec,
        scratch_shapes=[pltpu.VMEM((n_rows, d), F32), pltpu.VMEM((tm, d), BF16),
                        pltpu.SemaphoreType.DMA(())])
    return pl.pallas_call(
        functools.partial(_ffn_gather_kernel, tm=tm, sub=sub, rows_per_step=n_rows // nf),
        grid_spec=grid_spec, out_shape=out_shape,
        compiler_params=_cparams(("arbitrary", "arbitrary")), name="ffn_moe",
    )(tile_expert, tile_valid, src, src, h, wg, wu, wd)


def _conv_front_kernel(x1_ref, y_ref, moda_ref, modb_ref, g_ref, w_ref, b_ref,
                       x2_ref, glu_ref, *, d):
    x2 = x1_ref[0] + moda_ref[0][:, 2 * d:3 * d] * y_ref[0]
    x2_ref[0] = x2
    h = _adaln(x2, g_ref[...], modb_ref[0], d).astype(BF16)
    u = _dot(h, w_ref[...]) + b_ref[...]
    glu_ref[0] = u[:, 0:d] * jax.nn.sigmoid(u[:, d:2 * d])


def _conv_front_call(x1, y, moda, modb, g, w, bias, tm):
    b, s, d = x1.shape
    tile = lambda: pl.BlockSpec((1, tm, d), lambda bi, si: (bi, si, 0))
    modspec = lambda mod: _mod_spec(mod, d, lambda bi, si: bi)
    const = lambda a: pl.BlockSpec(a.shape, lambda bi, si: (0,) * a.ndim)
    return pl.pallas_call(
        functools.partial(_conv_front_kernel, d=d),
        grid=(b, s // tm),
        in_specs=[tile(), tile(), modspec(moda), modspec(modb), const(g), const(w),
                  const(bias)],
        out_specs=[tile(), tile()],
        out_shape=[jax.ShapeDtypeStruct((b, s, d), F32), jax.ShapeDtypeStruct((b, s, d), F32)],
        compiler_params=_cparams(("arbitrary", "arbitrary")),
        name="conv_front",
    )(x1, y, moda[0], modb[0], g, w, bias)


def _conv_back_kernel(glu_ref, halo_ref, x2_ref, moda_ref, modb_ref, wdw_ref, bdw_ref,
                      lng_ref, lnb_ref, w2_ref, b2_ref, g_ref, wrh_ref, wrl_ref,
                      x3_ref, h4_ref, route_ref, ext_ref, conv_ref, base_ref, sh_ref,
                      *, d, tm, rows, cols):
    bi = pl.program_id(0)
    si = pl.program_id(1)

    @pl.when(jnp.logical_and(bi == 0, si == 0))
    def _():
        base_ref[...] = jnp.zeros_like(base_ref)

    ext_ref[0:HALO, :] = jnp.where(si == 0, 0.0, halo_ref[0])
    ext_ref[HALO:HALO + tm, :] = glu_ref[0]

    off = HALO - (CONV_WIDTH - 1)
    span = sh_ref.shape[1]
    for c0 in range(0, d, cols):
        for r in range(1, SUBLANES):
            sh_ref[r - 1] = ext_ref[r:r + span, c0:c0 + cols]
        for r0 in range(0, tm, rows):
            groups = (rows // SUBLANES, SUBLANES, cols)
            acc = jnp.zeros(groups, F32) + bdw_ref[:, c0:c0 + cols][None]
            for k in range(CONV_WIDTH):
                r = (off + k) % SUBLANES
                a = r0 + off + k - r
                if r == 0:
                    tap = ext_ref[a:a + rows, c0:c0 + cols]
                else:
                    tap = sh_ref[r - 1, a:a + rows, :]
                acc = acc + wdw_ref[k, :, c0:c0 + cols][None] * tap.reshape(groups)
            conv_ref[r0:r0 + rows, c0:c0 + cols] = acc.reshape(rows, cols)

    u = conv_ref[...]
    mu = jnp.mean(u, axis=-1, keepdims=True)
    uc = u - mu
    var = jnp.mean(uc * uc, axis=-1, keepdims=True)
    u = uc * lax.rsqrt(var + EPS) * lng_ref[...] + lnb_ref[...]
    u = (u * jax.nn.sigmoid(u)).astype(BF16)
    y = _dot(u, w2_ref[...]) + b2_ref[...]
    x3 = x2_ref[0] + moda_ref[0][:, 2 * d:3 * d] * y
    x3_ref[0] = x3
    h4 = _adaln(x3, g_ref[...], modb_ref[0], d)
    h4_ref[0] = h4

    hh = h4.astype(BF16)
    hl = (h4 - hh.astype(F32)).astype(BF16)
    logits = _dot(hh, wrh_ref[...]) + _dot(hl, wrh_ref[...]) + _dot(hh, wrl_ref[...])
    lane = lax.broadcasted_iota(jnp.int32, (tm, LANES), 1).astype(F32)
    lg = jnp.where(lane < N_EXPERTS, logits, -jnp.inf)
    m1 = jnp.max(lg, axis=-1, keepdims=True)
    i1 = jnp.min(jnp.where(lg == m1, lane, float(LANES)), axis=-1, keepdims=True)
    lg2 = jnp.where(lane == i1, -jnp.inf, lg)
    m2 = jnp.max(lg2, axis=-1, keepdims=True)
    i2 = jnp.min(jnp.where(lg2 == m2, lane, float(LANES)), axis=-1, keepdims=True)
    e2 = jnp.exp(m2 - m1)
    w0 = 1.0 / (1.0 + e2)
    w1 = e2 / (1.0 + e2)
    oh0 = (lane == i1).astype(F32)
    oh1 = (lane == i2).astype(F32)
    both = oh0 + oh1
    tri = (lax.broadcasted_iota(jnp.int32, (tm, tm), 0)
           > lax.broadcasted_iota(jnp.int32, (tm, tm), 1)).astype(BF16)
    cnt = _dot(tri, both.astype(BF16)) + base_ref[...]
    r0 = jnp.sum(cnt * oh0, axis=-1, keepdims=True)
    r1 = jnp.sum(cnt * oh1, axis=-1, keepdims=True)
    base_ref[...] = base_ref[...] + jnp.sum(both, axis=0, keepdims=True)
    route = jnp.where(lane == 0.0, i1, 0.0)
    route = jnp.where(lane == 1.0, i2, route)
    route = jnp.where(lane == 2.0, w0, route)
    route = jnp.where(lane == 3.0, w1, route)
    route = jnp.where(lane == 4.0, r0, route)
    route = jnp.where(lane == 5.0, r1, route)
    route_ref[0] = route


def _conv_back_call(glu, x2, moda, modb, wdw, bdw, lng, lnb, w2, b2, g, wrh, wrl, tm):
    b, s, d = x2.shape
    tile = lambda: pl.BlockSpec((1, tm, d), lambda bi, si: (bi, si, 0))
    modspec = lambda mod: _mod_spec(mod, d, lambda bi, si: bi)
    const = lambda a: pl.BlockSpec(a.shape, lambda bi, si: (0,) * a.ndim)
    hpt = tm // HALO
    halo = pl.BlockSpec((1, HALO, d), lambda bi, si: (bi, jnp.maximum(si * hpt - 1, 0), 0))
    cols = 2 * LANES
    return pl.pallas_call(
        functools.partial(_conv_back_kernel, d=d, tm=tm, rows=64, cols=cols),
        grid=(b, s // tm),
        in_specs=[tile(), halo, tile(), modspec(moda), modspec(modb), const(wdw), const(bdw),
                  const(lng), const(lnb), const(w2), const(b2), const(g), const(wrh), const(wrl)],
        out_specs=[tile(), tile(), pl.BlockSpec((1, tm, LANES), lambda bi, si: (bi, si, 0))],
        out_shape=[jax.ShapeDtypeStruct((b, s, d), F32), jax.ShapeDtypeStruct((b, s, d), F32),
                   jax.ShapeDtypeStruct((b, s, LANES), F32)],
        scratch_shapes=[pltpu.VMEM((tm + HALO, d), F32), pltpu.VMEM((tm, d), F32),
                        pltpu.VMEM((1, LANES), F32),
                        pltpu.VMEM((SUBLANES - 1, tm + HALO - SUBLANES, cols), F32)],
        compiler_params=_cparams(("arbitrary", "arbitrary")),
        name="conv_back",
    )(glu, glu, x2, moda[0], modb[0], wdw, bdw, lng, lnb, w2, b2, g, wrh, wrl)


def _combine_kernel(p0_ref, p1_ref, p0n_ref, p1n_ref, x_ref, route_ref, mod_ref, fg_ref, ys_hbm,
                    o_ref, y_ref, sem, *, d, tm):
    i = pl.program_id(0)
    slot = i % 2

    def request(pa_ref, pb_ref, sl):
        def issue(r, c):
            pltpu.make_async_copy(ys_hbm.at[pl.ds(pa_ref[0, 0, r], 1), :],
                                  y_ref.at[sl, 0, pl.ds(r, 1), :], sem.at[sl, 0]).start()
            pltpu.make_async_copy(ys_hbm.at[pl.ds(pb_ref[0, 0, r], 1), :],
                                  y_ref.at[sl, 1, pl.ds(r, 1), :], sem.at[sl, 1]).start()
            return c
        lax.fori_loop(0, tm, issue, 0, unroll=8)

    @pl.when(i == 0)
    def _():
        request(p0_ref, p1_ref, 0)

    @pl.when(i + 1 < pl.num_programs(0))
    def _():
        request(p0n_ref, p1n_ref, 1 - slot)

    for pick in range(2):
        pltpu.make_async_copy(ys_hbm.at[pl.ds(0, tm), :], y_ref.at[slot, pick],
                              sem.at[slot, pick]).wait()
    w0 = route_ref[:, 2:3]
    w1 = route_ref[:, 3:4]
    y = w0 * y_ref[slot, 0] + w1 * y_ref[slot, 1]
    x = x_ref[...] + mod_ref[0][:, 2 * d:3 * d] * y
    o_ref[...] = _rms(x, fg_ref[...])


def _combine_call(p0, p1, x3, route, mod, fg, ys, tm, tiles_per_batch):
    t, d = x3.shape
    nt = t // tm
    nxt = lambda i: (jnp.minimum(i + 1, nt - 1), 0, 0)
    return pl.pallas_call(
        functools.partial(_combine_kernel, d=d, tm=tm),
        grid=(nt,),
        in_specs=[
            pl.BlockSpec((1, 1, tm), lambda i: (i, 0, 0), memory_space=pltpu.SMEM),
            pl.BlockSpec((1, 1, tm), lambda i: (i, 0, 0), memory_space=pltpu.SMEM),
            pl.BlockSpec((1, 1, tm), nxt, memory_space=pltpu.SMEM),
            pl.BlockSpec((1, 1, tm), nxt, memory_space=pltpu.SMEM),
            pl.BlockSpec((tm, d), lambda i: (i, 0)),
            pl.BlockSpec((tm, LANES), lambda i: (i, 0)),
            _mod_spec(mod, d, lambda i: i // tiles_per_batch),
            pl.BlockSpec(fg.shape, lambda i: (0, 0)),
            pl.BlockSpec(memory_space=pl.ANY),
        ],
        out_specs=pl.BlockSpec((tm, d), lambda i: (i, 0)),
        out_shape=jax.ShapeDtypeStruct((t, d), F32),
        scratch_shapes=[pltpu.VMEM((2, 2, tm, d), F32), pltpu.SemaphoreType.DMA((2, 2))],
        compiler_params=_cparams(("arbitrary",)),
        name="combine",
    )(p0, p1, p0, p1, x3, route, mod[0], fg, ys)


def _rope_consts():
    lane = np.arange(LANES)
    half = QK_ROPE // 2
    freq = np.where(lane < QK_ROPE, ROPE_THETA ** (-(2.0 * (lane % half)) / QK_ROPE), 0.0)
    sa = np.where((lane >= half) & (lane < QK_ROPE), 1.0, 0.0)
    sb = np.where(lane < half, -1.0, 0.0)
    rc = np.zeros((8, LANES), np.float32)
    rc[0], rc[1], rc[2] = freq, sa, sb
    return jnp.asarray(rc)


def _moe_plan(route, tm, n_tiles, n_rows):
    t = route.shape[0]
    assert n_tiles == (2 * t) // tm + N_EXPERTS and (2 * t) % tm == 0
    e0 = route[:, 0].astype(jnp.int32)
    e1 = route[:, 1].astype(jnp.int32)
    r0 = route[:, 4].astype(jnp.int32)
    r1 = route[:, 5].astype(jnp.int32)
    ids = jnp.arange(N_EXPERTS, dtype=jnp.int32)
    counts = (jnp.sum(e0[:, None] == ids[None, :], axis=0)
              + jnp.sum(e1[:, None] == ids[None, :], axis=0)).astype(jnp.int32)
    tiles = (counts + tm - 1) // tm
    tile_end = jnp.cumsum(tiles)
    tile_start = tile_end - tiles
    pos0 = tile_start[e0] * tm + r0
    pos1 = tile_start[e1] * tm + r1
    tok = jnp.arange(t, dtype=jnp.int32)
    pos = jnp.concatenate([pos0, pos1])
    src = jnp.zeros((n_tiles * n_rows,), jnp.int32).at[(pos // tm) * n_rows + pos % tm].set(
        jnp.concatenate([tok, tok]), unique_indices=True, mode="promise_in_bounds")
    ti = jnp.arange(n_tiles, dtype=jnp.int32)
    used = tile_end[-1]
    te = jnp.sum(tile_end[None, :] <= jnp.minimum(ti, used - 1)[:, None], axis=1)
    te = jnp.minimum(te, N_EXPERTS - 1).astype(jnp.int32)
    tv = jnp.clip(counts[te] - (ti - tile_start[te]) * tm, 0, tm)
    tv = jnp.where(ti < used, tv, 0).astype(jnp.int32)
    return pos0, pos1, src.reshape(n_tiles, 1, n_rows), te, tv


def kernel(x, c, positions, ada_w, ada_b, norm_g, mla_w_in, mla_g_q, mla_g_kv, mla_w_uq,
           mla_w_ukv, mla_w_out, conv_w_pw1, conv_b_pw1, conv_w_dw, conv_b_dw, conv_ln_g,
           conv_ln_b, conv_w_pw2, conv_b_pw2, ffn_w_gate, ffn_w_up, ffn_w_down, moe_w_router,
           moe_w_gate, moe_w_up, moe_w_down, final_g):
    b, s, d = x.shape
    t = b * s
    f = ffn_w_gate.shape[-1]
    assert ada_w.shape[0] == 2 and mla_w_in.shape[0] == 1 and conv_w_pw1.shape[0] == 1
    assert moe_w_gate.shape[1] == N_EXPERTS and conv_w_dw.shape[1] == CONV_WIDTH

    scale = (QK_NOPE + QK_ROPE) ** -0.5
    w_in = jnp.pad(mla_w_in[0], ((0, 0), (0, LANES - QK_ROPE))).astype(BF16)
    wq = mla_w_uq[0].reshape(Q_LORA, MLA_HEADS, QK_NOPE + QK_ROPE) * scale
    wq = jnp.pad(wq, ((0, 0), (0, 0), (0, HEAD_PAD - QK_NOPE - QK_ROPE)))
    wq = wq.reshape(Q_LORA, MLA_HEADS * HEAD_PAD).astype(BF16)
    wkv = mla_w_ukv[0].reshape(KV_LORA, MLA_HEADS, QK_NOPE + V_HEAD)
    wk = wkv[:, :, :QK_NOPE].reshape(KV_LORA, MLA_HEADS * QK_NOPE).astype(BF16)
    wv = wkv[:, :, QK_NOPE:].reshape(KV_LORA, MLA_HEADS * V_HEAD).astype(BF16)
    w_out = mla_w_out[0].astype(BF16)
    w_pw1 = conv_w_pw1[0].astype(BF16)
    w_pw2 = conv_w_pw2[0].astype(BF16)
    wr = jnp.pad(moe_w_router[0], ((0, 0), (0, LANES - N_EXPERTS)))
    wr_hi = wr.astype(BF16)
    wr_lo = (wr - wr_hi.astype(F32)).astype(BF16)
    row = lambda v: v.reshape(1, -1)

    mods = _ada_call(c, ada_w.reshape(4, d, 3 * d), ada_b.reshape(4, 3 * d))
    mods = mods.reshape(4, b, 1, 3 * d)
    mod = [(mods, i) for i in range(4)]

    q, k, v = _mla_front_call(x, positions.reshape(b, s, 1), mod[0], row(norm_g[0, 0]), w_in,
                              row(mla_g_q[0]), row(mla_g_kv[0]), wq, wk, wv, _rope_consts(),
                              tm=256)
    o = _attn_call(q, k, v, t=512, heads=4)
    x1, h2 = _attn_out_call(x, o, w_out, mod[0], mod[1], row(norm_g[0, 1]), tm=512)
    tm_d = 1024
    nt_d = t // tm_d
    y1 = _ffn_call(h2.reshape(t, d), ffn_w_gate.astype(BF16), ffn_w_up.astype(BF16),
                   ffn_w_down.astype(BF16), jnp.zeros((nt_d,), jnp.int32),
                   jnp.full((nt_d,), tm_d, jnp.int32), None, tm=tm_d, sub=tm_d, tf=512)

    x2, glu = _conv_front_call(x1, y1.reshape(b, s, d), mod[1], mod[2], row(norm_g[1, 0]),
                               w_pw1, row(conv_b_pw1[0]), tm=256)
    w_dw8 = jnp.broadcast_to(conv_w_dw[0][:, None, :], (CONV_WIDTH, SUBLANES, d))
    b_dw8 = jnp.broadcast_to(row(conv_b_dw[0]), (SUBLANES, d))
    x3, h4, route = _conv_back_call(glu, x2, mod[2], mod[3], w_dw8, b_dw8,
                                    row(conv_ln_g[0]), row(conv_ln_b[0]), w_pw2,
                                    row(conv_b_pw2[0]), row(norm_g[1, 1]), wr_hi, wr_lo, tm=256)
    route = route.reshape(t, LANES)
    tm_e = 1024
    tf_e = 512
    nt_e = (2 * t) // tm_e + N_EXPERTS
    nf_e = f // tf_e
    rps = -(-tm_e // nf_e)
    while (rps * nf_e) % SUBLANES:
        rps += 1
    rows_e = rps * nf_e
    pos0, pos1, src, te, tv = _moe_plan(route, tm_e, nt_e, rows_e)
    ys = _ffn_call(h4.reshape(t, d), moe_w_gate[0], moe_w_up[0], moe_w_down[0], te, tv, src,
                   tm=tm_e, sub=256, tf=tf_e)
    tm_c = 256
    out = _combine_call(pos0.reshape(t // tm_c, 1, tm_c), pos1.reshape(t // tm_c, 1, tm_c),
                        x3.reshape(t, d), route, mod[3], row(final_g), ys, tm=tm_c,
                        tiles_per_batch=s // tm_c)
    return out.reshape(b, s, d)
```

```python
import functools

import numpy as np
import jax
import jax.numpy as jnp
from jax import lax
from jax.experimental import pallas as pl
from jax.experimental.pallas import tpu as pltpu

F32 = jnp.float32
BF16 = jnp.bfloat16

MLA_HEADS = 16
Q_LORA = 512
KV_LORA = 512
QK_NOPE = 128
QK_ROPE = 64
V_HEAD = 128
ROPE_THETA = 10000.0
CONV_WIDTH = 31
N_EXPERTS = 8
EPS = 1e-6

LANES = 128
SUBLANES = 8
HEAD_PAD = 2 * LANES
HALO = 32
VMEM_LIMIT = 56 * 1024 * 1024
VMEM_LIMIT_MOE = 60 * 1024 * 1024

TILES = dict(
    mla_front=256, attn_block=512, attn_heads=4, attn_out=512, conv=256, combine=256,
    ffn_rows=1024, ffn_cols=512,
    moe_row_class=256,
    moe_weight_slabs=2,
)


def _cparams(sem, vmem=VMEM_LIMIT):
    return pltpu.CompilerParams(dimension_semantics=sem, vmem_limit_bytes=vmem)


def _rms(x, g):
    return x * lax.rsqrt(jnp.mean(x * x, axis=-1, keepdims=True) + EPS) * g


def _adaln(x, g, mod, d):
    return _rms(x, g) * (1.0 + mod[:, d:2 * d]) + mod[:, 0:d]


def _dot(a, b):
    return jnp.dot(a, b, preferred_element_type=F32)


def _mod_spec(mod, d, batch_of):
    k = mod[1]
    return pl.BlockSpec((None, 1, 1, 3 * d), lambda *g: (k, batch_of(*g), 0, 0))


def _ada_kernel(c_ref, w_ref, b_ref, o_ref):
    c = c_ref[...]
    sc = (c * jax.nn.sigmoid(c)).astype(BF16)
    o_ref[0] = _dot(sc, w_ref[0].astype(BF16)) + b_ref[0]


def _ada_call(c, ada_w, ada_b):
    n, d, d3 = ada_w.shape
    b = c.shape[0]
    tn = 1024
    return pl.pallas_call(
        _ada_kernel,
        grid=(n, d3 // tn),
        in_specs=[
            pl.BlockSpec((b, d), lambda i, j: (0, 0)),
            pl.BlockSpec((1, d, tn), lambda i, j: (i, 0, j)),
            pl.BlockSpec((1, 1, tn), lambda i, j: (i, 0, j)),
        ],
        out_specs=pl.BlockSpec((1, b, tn), lambda i, j: (i, 0, j)),
        out_shape=jax.ShapeDtypeStruct((n, b, d3), F32),
        compiler_params=_cparams(("arbitrary", "arbitrary")),
        name="ada",
    )(c, ada_w, ada_b.reshape(n, 1, d3))


def _rope_tile(t, cosv, sa, sb):
    return t * cosv + pltpu.roll(t, 32, 1) * sa + pltpu.roll(t, 96, 1) * sb


def _mla_front_kernel(x_ref, pos_ref, mod_ref, g_ref, win_ref, gq_ref, gkv_ref,
                      wq_ref, wk_ref, wv_ref, rc_ref, q_ref, k_ref, v_ref, *, d):
    x = x_ref[0]
    h = _adaln(x, g_ref[...], mod_ref[0], d).astype(BF16)
    lat = _dot(h, win_ref[...])
    cq = _rms(lat[:, 0:Q_LORA], gq_ref[...]).astype(BF16)
    ckv = _rms(lat[:, Q_LORA:Q_LORA + KV_LORA], gkv_ref[...]).astype(BF16)
    kr = lat[:, Q_LORA + KV_LORA:Q_LORA + KV_LORA + LANES]

    ang = pos_ref[0].astype(F32) * rc_ref[0:1, :]
    cosv = jnp.cos(ang)
    sinv = jnp.sin(ang)
    sa = sinv * rc_ref[1:2, :]
    sb = sinv * rc_ref[2:3, :]

    q = _dot(cq, wq_ref[...])
    for hh in range(MLA_HEADS):
        lo = hh * HEAD_PAD
        q_ref[0, :, lo:lo + LANES] = q[:, lo:lo + LANES].astype(BF16)
        q_ref[0, :, lo + LANES:lo + HEAD_PAD] = _rope_tile(
            q[:, lo + LANES:lo + HEAD_PAD], cosv, sa, sb).astype(BF16)

    kn = _dot(ckv, wk_ref[...])
    krr = _rope_tile(kr, cosv, sa, sb).astype(BF16)
    for hh in range(MLA_HEADS):
        lo = hh * HEAD_PAD
        k_ref[0, :, lo:lo + LANES] = kn[:, hh * LANES:(hh + 1) * LANES].astype(BF16)
        k_ref[0, :, lo + LANES:lo + HEAD_PAD] = krr
    v_ref[0] = _dot(ckv, wv_ref[...]).astype(BF16)


def _mla_front_call(x, pos3, mod, g, win, gq, gkv, wq, wk, wv, rc, tm):
    b, s, d = x.shape
    hq = MLA_HEADS * HEAD_PAD
    hv = MLA_HEADS * V_HEAD
    const = lambda shape: pl.BlockSpec(shape, lambda bi, si: (0,) * len(shape))
    return pl.pallas_call(
        functools.partial(_mla_front_kernel, d=d),
        grid=(b, s // tm),
        in_specs=[
            pl.BlockSpec((1, tm, d), lambda bi, si: (bi, si, 0)),
            pl.BlockSpec((1, tm, 1), lambda bi, si: (bi, si, 0)),
            _mod_spec(mod, d, lambda bi, si: bi),
            const(g.shape), const(win.shape), const(gq.shape), const(gkv.shape),
            const(wq.shape), const(wk.shape), const(wv.shape), const(rc.shape),
        ],
        out_specs=[
            pl.BlockSpec((1, tm, hq), lambda bi, si: (bi, si, 0)),
            pl.BlockSpec((1, tm, hq), lambda bi, si: (bi, si, 0)),
            pl.BlockSpec((1, tm, hv), lambda bi, si: (bi, si, 0)),
        ],
        out_shape=[
            jax.ShapeDtypeStruct((b, s, hq), BF16),
            jax.ShapeDtypeStruct((b, s, hq), BF16),
            jax.ShapeDtypeStruct((b, s, hv), BF16),
        ],
        compiler_params=_cparams(("arbitrary", "arbitrary")),
        name="mla_front",
    )(x, pos3, mod[0], g, win, gq, gkv, wq, wk, wv, rc)


def _attn_kernel(q_ref, k_ref, v_ref, o_ref, *, t, heads):
    qi = pl.program_id(2)
    tril = (lax.broadcasted_iota(jnp.int32, (t, t), 1)
            <= lax.broadcasted_iota(jnp.int32, (t, t), 0))

    def block(j, carry, masked):
        start = pl.multiple_of(j * t, t)
        out = []
        for hh in range(heads):
            m, l, acc = carry[hh]
            q = q_ref[0, :, hh * HEAD_PAD:(hh + 1) * HEAD_PAD]
            k = k_ref[0, pl.ds(start, t), hh * HEAD_PAD:(hh + 1) * HEAD_PAD]
            v = v_ref[0, pl.ds(start, t), hh * V_HEAD:(hh + 1) * V_HEAD]
            s = lax.dot_general(q, k, (((1,), (1,)), ((), ())), preferred_element_type=F32)
            if masked:
                s = jnp.where(tril, s, -jnp.inf)
            m_new = jnp.maximum(m, jnp.max(s, axis=-1, keepdims=True))
            alpha = jnp.exp(m - m_new)
            p = jnp.exp(s - m_new)
            l = alpha * l + jnp.sum(p, axis=-1, keepdims=True)
            acc = alpha * acc + _dot(p.astype(BF16), v)
            out.append((m_new, l, acc))
        return tuple(out)

    init = tuple((jnp.full((t, 1), -jnp.inf, F32), jnp.zeros((t, 1), F32),
                  jnp.zeros((t, V_HEAD), F32)) for _ in range(heads))
    carry = lax.fori_loop(0, qi, functools.partial(block, masked=False), init)
    carry = block(qi, carry, True)
    for hh in range(heads):
        _, l, acc = carry[hh]
        o_ref[0, :, hh * V_HEAD:(hh + 1) * V_HEAD] = (acc / l).astype(BF16)


def _attn_call(q, k, v, t, heads):
    b, s, _ = q.shape
    return pl.pallas_call(
        functools.partial(_attn_kernel, t=t, heads=heads),
        grid=(b, MLA_HEADS // heads, s // t),
        in_specs=[
            pl.BlockSpec((1, t, heads * HEAD_PAD), lambda bi, hi, qi: (bi, qi, hi)),
            pl.BlockSpec((1, s, heads * HEAD_PAD), lambda bi, hi, qi: (bi, 0, hi)),
            pl.BlockSpec((1, s, heads * V_HEAD), lambda bi, hi, qi: (bi, 0, hi)),
        ],
        out_specs=pl.BlockSpec((1, t, heads * V_HEAD), lambda bi, hi, qi: (bi, qi, hi)),
        out_shape=jax.ShapeDtypeStruct((b, s, MLA_HEADS * V_HEAD), BF16),
        compiler_params=_cparams(("arbitrary", "arbitrary", "arbitrary")),
        name="attention",
    )(q, k, v)


def _attn_out_kernel(x_ref, o_ref, w_ref, moda_ref, modb_ref, g_ref, x1_ref, h_ref, *, d):
    y = _dot(o_ref[0], w_ref[...])
    x1 = x_ref[0] + moda_ref[0][:, 2 * d:3 * d] * y
    x1_ref[0] = x1
    h_ref[0] = _adaln(x1, g_ref[...], modb_ref[0], d).astype(BF16)


def _attn_out_call(x, o, w, moda, modb, g, tm):
    b, s, d = x.shape
    tile = lambda: pl.BlockSpec((1, tm, d), lambda bi, si: (bi, si, 0))
    modspec = lambda mod: _mod_spec(mod, d, lambda bi, si: bi)
    return pl.pallas_call(
        functools.partial(_attn_out_kernel, d=d),
        grid=(b, s // tm),
        in_specs=[tile(), tile(), pl.BlockSpec(w.shape, lambda bi, si: (0, 0)),
                  modspec(moda), modspec(modb), pl.BlockSpec(g.shape, lambda bi, si: (0, 0))],
        out_specs=[tile(), tile()],
        out_shape=[jax.ShapeDtypeStruct((b, s, d), F32), jax.ShapeDtypeStruct((b, s, d), BF16)],
        compiler_params=_cparams(("arbitrary", "arbitrary")),
        name="attn_out",
    )(x, o, w, moda[0], modb[0], g)


def _dot_parts(x, parts):
    kp = x.shape[1] // len(parts)
    out = None
    for p, ref in enumerate(parts):
        t = _dot(x[:, p * kp:(p + 1) * kp], ref[0].astype(BF16))
        out = t if out is None else out + t
    return out


def _ffn_body(valid, h_bf, w_refs, o_ref, tm, sub, extra=None):
    n_parts = len(w_refs) // 3
    wg, wu, wd = (w_refs[k * n_parts:(k + 1) * n_parts] for k in range(3))
    n_sub = tm // sub
    for n in range(1, n_sub + 1):
        lo = (n - 1) * sub
        cond = valid > lo if n == n_sub else jnp.logical_and(valid > lo, valid <= n * sub)

        @pl.when(cond)
        def _():
            if extra is not None:
                extra()
            rows = slice(0, n * sub)
            h = h_bf[rows, :]
            g = _dot_parts(h, wg)
            u = _dot_parts(h, wu)
            a = (g * jax.nn.sigmoid(g) * u).astype(BF16)
            o_ref[rows, :] += _dot_parts(a, wd)


def _ffn_dense_kernel(te_ref, tv_ref, h_ref, *rest, tm, sub, n_w):
    w_refs, o_ref = rest[:n_w], rest[n_w]
    i = pl.program_id(0)
    j = pl.program_id(1)

    @pl.when(j == 0)
    def _():
        o_ref[...] = jnp.zeros_like(o_ref)

    _ffn_body(tv_ref[i], h_ref, w_refs, o_ref, tm, sub)


def _ffn_gather_kernel(te_ref, tv_ref, src_ref, srcn_ref, h_hbm, *rest, tm, sub, rows_per_step,
                       n_w):
    w_refs = rest[:n_w]
    o_ref, hf_ref, hb_ref, sem = rest[n_w:]
    i = pl.program_id(0)
    j = pl.program_id(1)
    valid = tv_ref[i]
    n_rows = hf_ref.shape[0]

    def row_copy(idx_ref, r):
        return pltpu.make_async_copy(h_hbm.at[pl.ds(idx_ref[0, 0, r], 1), :],
                                     hf_ref.at[pl.ds(r, 1), :], sem)

    def wait_rows():
        pltpu.make_async_copy(h_hbm.at[pl.ds(0, n_rows), :], hf_ref, sem).wait()

    @pl.when(j == 0)
    def _():
        o_ref[...] = jnp.zeros_like(o_ref)

        @pl.when(jnp.logical_and(i == 0, valid > 0))
        def _():
            def issue(r, c):
                row_copy(src_ref, r).start()
                return c
            lax.fori_loop(0, n_rows, issue, 0)
            wait_rows()

        @pl.when(jnp.logical_and(i > 0, tv_ref[jnp.maximum(i - 1, 0)] > 0))
        def _():
            wait_rows()

        @pl.when(valid > 0)
        def _():
            hb_ref[...] = hf_ref[0:tm, :].astype(BF16)

    def request_next_rows():
        for q in range(rows_per_step):
            row_copy(srcn_ref, j * rows_per_step + q).start()

    _ffn_body(valid, hb_ref, w_refs, o_ref, tm, sub, request_next_rows)


def _ffn_call(h, wg, wu, wd, tile_expert, tile_valid, src, tm, sub, tf, parts):
    d = wg.shape[1]
    f = wg.shape[2]
    nf = f // tf
    nt = tile_expert.shape[0]

    def jj(i, j, tv):
        return jnp.where(tv[i] > 0, j, nf - 1)

    up_spec = lambda p: pl.BlockSpec(
        (1, d // parts, tf), lambda i, j, te, tv: (te[i], p, jj(i, j, tv)))
    down_spec = lambda p: pl.BlockSpec(
        (1, tf // parts, d), lambda i, j, te, tv: (te[i], jj(i, j, tv) * parts + p, 0))
    w_specs = ([up_spec(p) for p in range(parts)] + [up_spec(p) for p in range(parts)]
               + [down_spec(p) for p in range(parts)])
    w_args = [wg] * parts + [wu] * parts + [wd] * parts
    n_w = len(w_specs)
    out_spec = pl.BlockSpec((tm, d), lambda i, j, te, tv: (i, 0),
                            pipeline_mode=pl.Buffered(1) if src is not None else None)
    out_shape = jax.ShapeDtypeStruct((nt * tm, d), F32)
    if src is None:
        grid_spec = pltpu.PrefetchScalarGridSpec(
            num_scalar_prefetch=2, grid=(nt, nf),
            in_specs=[pl.BlockSpec((tm, d), lambda i, j, te, tv: (i, 0))] + w_specs,
            out_specs=out_spec)
        return pl.pallas_call(
            functools.partial(_ffn_dense_kernel, tm=tm, sub=sub, n_w=n_w), grid_spec=grid_spec,
            out_shape=out_shape,
            compiler_params=_cparams(("arbitrary", "arbitrary")), name="ffn_dense",
        )(tile_expert, tile_valid, h, *w_args)
    n_rows = src.shape[2]
    assert n_rows % nf == 0 and n_rows >= tm
    grid_spec = pltpu.PrefetchScalarGridSpec(
        num_scalar_prefetch=2, grid=(nt, nf),
        in_specs=[pl.BlockSpec((1, 1, n_rows), lambda i, j, te, tv: (i, 0, 0),
                               memory_space=pltpu.SMEM),
                  pl.BlockSpec((1, 1, n_rows),
                               lambda i, j, te, tv: (jnp.minimum(i + 1, nt - 1), 0, 0),
                               memory_space=pltpu.SMEM),
                  pl.BlockSpec(memory_space=pl.ANY)] + w_specs,
        out_specs=out_spec,
        scratch_shapes=[pltpu.VMEM((n_rows, d), F32), pltpu.VMEM((tm, d), BF16),
                        pltpu.SemaphoreType.DMA(())])
    return pl.pallas_call(
        functools.partial(_ffn_gather_kernel, tm=tm, sub=sub, rows_per_step=n_rows // nf,
                          n_w=n_w),
        grid_spec=grid_spec, out_shape=out_shape,
        compiler_params=_cparams(("arbitrary", "arbitrary"), VMEM_LIMIT_MOE), name="ffn_moe",
    )(tile_expert, tile_valid, src, src, h, *w_args)


def _conv_front_kernel(x1_ref, y_ref, moda_ref, modb_ref, g_ref, w_ref, b_ref,
                       x2_ref, glu_ref, *, d):
    x2 = x1_ref[0] + moda_ref[0][:, 2 * d:3 * d] * y_ref[0]
    x2_ref[0] = x2
    h = _adaln(x2, g_ref[...], modb_ref[0], d).astype(BF16)
    u = _dot(h, w_ref[...]) + b_ref[...]
    glu_ref[0] = u[:, 0:d] * jax.nn.sigmoid(u[:, d:2 * d])


def _conv_front_call(x1, y, moda, modb, g, w, bias, tm):
    b, s, d = x1.shape
    tile = lambda: pl.BlockSpec((1, tm, d), lambda bi, si: (bi, si, 0))
    modspec = lambda mod: _mod_spec(mod, d, lambda bi, si: bi)
    const = lambda a: pl.BlockSpec(a.shape, lambda bi, si: (0,) * a.ndim)
    return pl.pallas_call(
        functools.partial(_conv_front_kernel, d=d),
        grid=(b, s // tm),
        in_specs=[tile(), tile(), modspec(moda), modspec(modb), const(g), const(w),
                  const(bias)],
        out_specs=[tile(), tile()],
        out_shape=[jax.ShapeDtypeStruct((b, s, d), F32), jax.ShapeDtypeStruct((b, s, d), F32)],
        compiler_params=_cparams(("arbitrary", "arbitrary")),
        name="conv_front",
    )(x1, y, moda[0], modb[0], g, w, bias)


def _conv_back_kernel(glu_ref, halo_ref, x2_ref, moda_ref, modb_ref, wdw_ref, bdw_ref,
                      lng_ref, lnb_ref, w2_ref, b2_ref, g_ref, wrh_ref, wrl_ref,
                      x3_ref, h4_ref, route_ref, ext_ref, conv_ref, base_ref, sh_ref,
                      *, d, tm, rows, cols):
    bi = pl.program_id(0)
    si = pl.program_id(1)

    @pl.when(jnp.logical_and(bi == 0, si == 0))
    def _():
        base_ref[...] = jnp.zeros_like(base_ref)

    ext_ref[0:HALO, :] = jnp.where(si == 0, 0.0, halo_ref[0])
    ext_ref[HALO:HALO + tm, :] = glu_ref[0]

    off = HALO - (CONV_WIDTH - 1)
    span = sh_ref.shape[1]
    for c0 in range(0, d, cols):
        for r in range(1, SUBLANES):
            sh_ref[r - 1] = ext_ref[r:r + span, c0:c0 + cols]
        for r0 in range(0, tm, rows):
            groups = (rows // SUBLANES, SUBLANES, cols)
            acc = jnp.zeros(groups, F32) + bdw_ref[:, c0:c0 + cols][None]
            for k in range(CONV_WIDTH):
                r = (off + k) % SUBLANES
                a = r0 + off + k - r
                if r == 0:
                    tap = ext_ref[a:a + rows, c0:c0 + cols]
                else:
                    tap = sh_ref[r - 1, a:a + rows, :]
                acc = acc + wdw_ref[k, :, c0:c0 + cols][None] * tap.reshape(groups)
            conv_ref[r0:r0 + rows, c0:c0 + cols] = acc.reshape(rows, cols)

    u = conv_ref[...]
    mu = jnp.mean(u, axis=-1, keepdims=True)
    uc = u - mu
    var = jnp.mean(uc * uc, axis=-1, keepdims=True)
    u = uc * lax.rsqrt(var + EPS) * lng_ref[...] + lnb_ref[...]
    u = (u * jax.nn.sigmoid(u)).astype(BF16)
    y = _dot(u, w2_ref[...]) + b2_ref[...]
    x3 = x2_ref[0] + moda_ref[0][:, 2 * d:3 * d] * y
    x3_ref[0] = x3
    h4 = _adaln(x3, g_ref[...], modb_ref[0], d)
    h4_ref[0] = h4

    hh = h4.astype(BF16)
    hl = (h4 - hh.astype(F32)).astype(BF16)
    logits = _dot(hh, wrh_ref[...]) + _dot(hl, wrh_ref[...]) + _dot(hh, wrl_ref[...])
    lane = lax.broadcasted_iota(jnp.int32, (tm, LANES), 1).astype(F32)
    lg = jnp.where(lane < N_EXPERTS, logits, -jnp.inf)
    m1 = jnp.max(lg, axis=-1, keepdims=True)
    i1 = jnp.min(jnp.where(lg == m1, lane, float(LANES)), axis=-1, keepdims=True)
    lg2 = jnp.where(lane == i1, -jnp.inf, lg)
    m2 = jnp.max(lg2, axis=-1, keepdims=True)
    i2 = jnp.min(jnp.where(lg2 == m2, lane, float(LANES)), axis=-1, keepdims=True)
    e2 = jnp.exp(m2 - m1)
    w0 = 1.0 / (1.0 + e2)
    w1 = e2 / (1.0 + e2)
    oh0 = (lane == i1).astype(F32)
    oh1 = (lane == i2).astype(F32)
    both = oh0 + oh1
    tri = (lax.broadcasted_iota(jnp.int32, (tm, tm), 0)
           > lax.broadcasted_iota(jnp.int32, (tm, tm), 1)).astype(BF16)
    cnt = _dot(tri, both.astype(BF16)) + base_ref[...]
    r0 = jnp.sum(cnt * oh0, axis=-1, keepdims=True)
    r1 = jnp.sum(cnt * oh1, axis=-1, keepdims=True)
    base_ref[...] = base_ref[...] + jnp.sum(both, axis=0, keepdims=True)
    route = jnp.where(lane == 0.0, i1, 0.0)
    route = jnp.where(lane == 1.0, i2, route)
    route = jnp.where(lane == 2.0, w0, route)
    route = jnp.where(lane == 3.0, w1, route)
    route = jnp.where(lane == 4.0, r0, route)
    route = jnp.where(lane == 5.0, r1, route)
    route_ref[0] = route


def _conv_back_call(glu, x2, moda, modb, wdw, bdw, lng, lnb, w2, b2, g, wrh, wrl, tm):
    b, s, d = x2.shape
    tile = lambda: pl.BlockSpec((1, tm, d), lambda bi, si: (bi, si, 0))
    modspec = lambda mod: _mod_spec(mod, d, lambda bi, si: bi)
    const = lambda a: pl.BlockSpec(a.shape, lambda bi, si: (0,) * a.ndim)
    hpt = tm // HALO
    halo = pl.BlockSpec((1, HALO, d), lambda bi, si: (bi, jnp.maximum(si * hpt - 1, 0), 0))
    cols = 2 * LANES
    return pl.pallas_call(
        functools.partial(_conv_back_kernel, d=d, tm=tm, rows=64, cols=cols),
        grid=(b, s // tm),
        in_specs=[tile(), halo, tile(), modspec(moda), modspec(modb), const(wdw), const(bdw),
                  const(lng), const(lnb), const(w2), const(b2), const(g), const(wrh), const(wrl)],
        out_specs=[tile(), tile(), pl.BlockSpec((1, tm, LANES), lambda bi, si: (bi, si, 0))],
        out_shape=[jax.ShapeDtypeStruct((b, s, d), F32), jax.ShapeDtypeStruct((b, s, d), F32),
                   jax.ShapeDtypeStruct((b, s, LANES), F32)],
        scratch_shapes=[pltpu.VMEM((tm + HALO, d), F32), pltpu.VMEM((tm, d), F32),
                        pltpu.VMEM((1, LANES), F32),
                        pltpu.VMEM((SUBLANES - 1, tm + HALO - SUBLANES, cols), F32)],
        compiler_params=_cparams(("arbitrary", "arbitrary")),
        name="conv_back",
    )(glu, glu, x2, moda[0], modb[0], wdw, bdw, lng, lnb, w2, b2, g, wrh, wrl)


def _combine_kernel(p0_ref, p1_ref, p0n_ref, p1n_ref, x_ref, route_ref, mod_ref, fg_ref, ys_hbm,
                    o_ref, y_ref, sem, *, d, tm):
    i = pl.program_id(0)
    slot = i % 2

    def request(pa_ref, pb_ref, sl):
        def issue(r, c):
            pltpu.make_async_copy(ys_hbm.at[pl.ds(pa_ref[0, 0, r], 1), :],
                                  y_ref.at[sl, 0, pl.ds(r, 1), :], sem.at[sl, 0]).start()
            pltpu.make_async_copy(ys_hbm.at[pl.ds(pb_ref[0, 0, r], 1), :],
                                  y_ref.at[sl, 1, pl.ds(r, 1), :], sem.at[sl, 1]).start()
            return c
        lax.fori_loop(0, tm, issue, 0, unroll=8)

    @pl.when(i == 0)
    def _():
        request(p0_ref, p1_ref, 0)

    @pl.when(i + 1 < pl.num_programs(0))
    def _():
        request(p0n_ref, p1n_ref, 1 - slot)

    for pick in range(2):
        pltpu.make_async_copy(ys_hbm.at[pl.ds(0, tm), :], y_ref.at[slot, pick],
                              sem.at[slot, pick]).wait()
    w0 = route_ref[:, 2:3]
    w1 = route_ref[:, 3:4]
    y = w0 * y_ref[slot, 0] + w1 * y_ref[slot, 1]
    x = x_ref[...] + mod_ref[0][:, 2 * d:3 * d] * y
    o_ref[...] = _rms(x, fg_ref[...])


def _combine_call(p0, p1, x3, route, mod, fg, ys, tm, tiles_per_batch):
    t, d = x3.shape
    nt = t // tm
    nxt = lambda i: (jnp.minimum(i + 1, nt - 1), 0, 0)
    return pl.pallas_call(
        functools.partial(_combine_kernel, d=d, tm=tm),
        grid=(nt,),
        in_specs=[
            pl.BlockSpec((1, 1, tm), lambda i: (i, 0, 0), memory_space=pltpu.SMEM),
            pl.BlockSpec((1, 1, tm), lambda i: (i, 0, 0), memory_space=pltpu.SMEM),
            pl.BlockSpec((1, 1, tm), nxt, memory_space=pltpu.SMEM),
            pl.BlockSpec((1, 1, tm), nxt, memory_space=pltpu.SMEM),
            pl.BlockSpec((tm, d), lambda i: (i, 0)),
            pl.BlockSpec((tm, LANES), lambda i: (i, 0)),
            _mod_spec(mod, d, lambda i: i // tiles_per_batch),
            pl.BlockSpec(fg.shape, lambda i: (0, 0)),
            pl.BlockSpec(memory_space=pl.ANY),
        ],
        out_specs=pl.BlockSpec((tm, d), lambda i: (i, 0)),
        out_shape=jax.ShapeDtypeStruct((t, d), F32),
        scratch_shapes=[pltpu.VMEM((2, 2, tm, d), F32), pltpu.SemaphoreType.DMA((2, 2))],
        compiler_params=_cparams(("arbitrary",)),
        name="combine",
    )(p0, p1, p0, p1, x3, route, mod[0], fg, ys)


def _rope_consts():
    lane = np.arange(LANES)
    half = QK_ROPE // 2
    freq = np.where(lane < QK_ROPE, ROPE_THETA ** (-(2.0 * (lane % half)) / QK_ROPE), 0.0)
    sa = np.where((lane >= half) & (lane < QK_ROPE), 1.0, 0.0)
    sb = np.where(lane < half, -1.0, 0.0)
    rc = np.zeros((8, LANES), np.float32)
    rc[0], rc[1], rc[2] = freq, sa, sb
    return jnp.asarray(rc)


def _moe_plan(route, tm, n_tiles, n_rows):
    t = route.shape[0]
    assert n_tiles == (2 * t) // tm + N_EXPERTS and (2 * t) % tm == 0
    e0 = route[:, 0].astype(jnp.int32)
    e1 = route[:, 1].astype(jnp.int32)
    r0 = route[:, 4].astype(jnp.int32)
    r1 = route[:, 5].astype(jnp.int32)
    ids = jnp.arange(N_EXPERTS, dtype=jnp.int32)
    counts = (jnp.sum(e0[:, None] == ids[None, :], axis=0)
              + jnp.sum(e1[:, None] == ids[None, :], axis=0)).astype(jnp.int32)
    tiles = (counts + tm - 1) // tm
    tile_end = jnp.cumsum(tiles)
    tile_start = tile_end - tiles
    pos0 = tile_start[e0] * tm + r0
    pos1 = tile_start[e1] * tm + r1
    tok = jnp.arange(t, dtype=jnp.int32)
    pos = jnp.concatenate([pos0, pos1])
    src = jnp.zeros((n_tiles * n_rows,), jnp.int32).at[(pos // tm) * n_rows + pos % tm].set(
        jnp.concatenate([tok, tok]), unique_indices=True, mode="promise_in_bounds")
    ti = jnp.arange(n_tiles, dtype=jnp.int32)
    used = tile_end[-1]
    te = jnp.sum(tile_end[None, :] <= jnp.minimum(ti, used - 1)[:, None], axis=1)
    te = jnp.minimum(te, N_EXPERTS - 1).astype(jnp.int32)
    tv = jnp.clip(counts[te] - (ti - tile_start[te]) * tm, 0, tm)
    tv = jnp.where(ti < used, tv, 0).astype(jnp.int32)
    return pos0, pos1, src.reshape(n_tiles, 1, n_rows), te, tv


def kernel(x, c, positions, ada_w, ada_b, norm_g, mla_w_in, mla_g_q, mla_g_kv, mla_w_uq,
           mla_w_ukv, mla_w_out, conv_w_pw1, conv_b_pw1, conv_w_dw, conv_b_dw, conv_ln_g,
           conv_ln_b, conv_w_pw2, conv_b_pw2, ffn_w_gate, ffn_w_up, ffn_w_down, moe_w_router,
           moe_w_gate, moe_w_up, moe_w_down, final_g):
    b, s, d = x.shape
    t = b * s
    f = ffn_w_gate.shape[-1]
    assert ada_w.shape[0] == 2 and mla_w_in.shape[0] == 1 and conv_w_pw1.shape[0] == 1
    assert moe_w_gate.shape[1] == N_EXPERTS and conv_w_dw.shape[1] == CONV_WIDTH

    scale = (QK_NOPE + QK_ROPE) ** -0.5
    w_in = jnp.pad(mla_w_in[0], ((0, 0), (0, LANES - QK_ROPE))).astype(BF16)
    wq = mla_w_uq[0].reshape(Q_LORA, MLA_HEADS, QK_NOPE + QK_ROPE) * scale
    wq = jnp.pad(wq, ((0, 0), (0, 0), (0, HEAD_PAD - QK_NOPE - QK_ROPE)))
    wq = wq.reshape(Q_LORA, MLA_HEADS * HEAD_PAD).astype(BF16)
    wkv = mla_w_ukv[0].reshape(KV_LORA, MLA_HEADS, QK_NOPE + V_HEAD)
    wk = wkv[:, :, :QK_NOPE].reshape(KV_LORA, MLA_HEADS * QK_NOPE).astype(BF16)
    wv = wkv[:, :, QK_NOPE:].reshape(KV_LORA, MLA_HEADS * V_HEAD).astype(BF16)
    w_out = mla_w_out[0].astype(BF16)
    w_pw1 = conv_w_pw1[0].astype(BF16)
    w_pw2 = conv_w_pw2[0].astype(BF16)
    wr = jnp.pad(moe_w_router[0], ((0, 0), (0, LANES - N_EXPERTS)))
    wr_hi = wr.astype(BF16)
    wr_lo = (wr - wr_hi.astype(F32)).astype(BF16)
    row = lambda v: v.reshape(1, -1)

    mods = _ada_call(c, ada_w.reshape(4, d, 3 * d), ada_b.reshape(4, 3 * d))
    mods = mods.reshape(4, b, 1, 3 * d)
    mod = [(mods, i) for i in range(4)]

    q, k, v = _mla_front_call(x, positions.reshape(b, s, 1), mod[0], row(norm_g[0, 0]), w_in,
                              row(mla_g_q[0]), row(mla_g_kv[0]), wq, wk, wv, _rope_consts(),
                              tm=TILES["mla_front"])
    o = _attn_call(q, k, v, t=TILES["attn_block"], heads=TILES["attn_heads"])
    x1, h2 = _attn_out_call(x, o, w_out, mod[0], mod[1], row(norm_g[0, 1]),
                            tm=TILES["attn_out"])
    tm_d = TILES["ffn_rows"]
    nt_d = t // tm_d
    y1 = _ffn_call(h2.reshape(t, d), ffn_w_gate.astype(BF16), ffn_w_up.astype(BF16),
                   ffn_w_down.astype(BF16), jnp.zeros((nt_d,), jnp.int32),
                   jnp.full((nt_d,), tm_d, jnp.int32), None, tm=tm_d, sub=tm_d,
                   tf=TILES["ffn_cols"], parts=1)

    x2, glu = _conv_front_call(x1, y1.reshape(b, s, d), mod[1], mod[2], row(norm_g[1, 0]),
                               w_pw1, row(conv_b_pw1[0]), tm=TILES["conv"])
    w_dw8 = jnp.broadcast_to(conv_w_dw[0][:, None, :], (CONV_WIDTH, SUBLANES, d))
    b_dw8 = jnp.broadcast_to(row(conv_b_dw[0]), (SUBLANES, d))
    x3, h4, route = _conv_back_call(glu, x2, mod[2], mod[3], w_dw8, b_dw8,
                                    row(conv_ln_g[0]), row(conv_ln_b[0]), w_pw2,
                                    row(conv_b_pw2[0]), row(norm_g[1, 1]), wr_hi, wr_lo,
                                    tm=TILES["conv"])
    route = route.reshape(t, LANES)
    tm_e = TILES["ffn_rows"]
    tf_e = TILES["ffn_cols"]
    nt_e = (2 * t) // tm_e + N_EXPERTS
    nf_e = f // tf_e
    rps = -(-tm_e // nf_e)
    while (rps * nf_e) % SUBLANES:
        rps += 1
    rows_e = rps * nf_e
    pos0, pos1, src, te, tv = _moe_plan(route, tm_e, nt_e, rows_e)
    ys = _ffn_call(h4.reshape(t, d), moe_w_gate[0], moe_w_up[0], moe_w_down[0], te, tv, src,
                   tm=tm_e, sub=TILES["moe_row_class"], tf=tf_e,
                   parts=TILES["moe_weight_slabs"])
    tm_c = TILES["combine"]
    out = _combine_call(pos0.reshape(t // tm_c, 1, tm_c), pos1.reshape(t // tm_c, 1, tm_c),
                        x3.reshape(t, d), route, mod[3], row(final_g), ys, tm=tm_c,
                        tiles_per_batch=s // tm_c)
    return out.reshape(b, s, d)
```

```python
import functools

import numpy as np
import jax
import jax.numpy as jnp
from jax import lax
from jax.experimental import pallas as pl
from jax.experimental.pallas import tpu as pltpu

F32 = jnp.float32
BF16 = jnp.bfloat16

MLA_HEADS = 16
Q_LORA = 512
KV_LORA = 512
QK_NOPE = 128
QK_ROPE = 64
V_HEAD = 128
ROPE_THETA = 10000.0
CONV_WIDTH = 31
N_EXPERTS = 8
EPS = 1e-6

LANES = 128
SUBLANES = 8
HEAD_PAD = 2 * LANES
HALO = 32
VMEM_LIMIT = 56 * 1024 * 1024
VMEM_LIMIT_MOE = 60 * 1024 * 1024

TILES = dict(
    mla_front=256, attn_block=512, attn_heads=4, attn_out=512, conv=256, combine=256,
    ffn_rows=1024, ffn_cols=512,
    moe_row_class=256,
)


def _cparams(sem, vmem=VMEM_LIMIT):
    return pltpu.CompilerParams(dimension_semantics=sem, vmem_limit_bytes=vmem)


def _rms(x, g):
    return x * lax.rsqrt(jnp.mean(x * x, axis=-1, keepdims=True) + EPS) * g


def _adaln(x, g, mod, d):
    return _rms(x, g) * (1.0 + mod[:, d:2 * d]) + mod[:, 0:d]


def _dot(a, b):
    return jnp.dot(a, b, preferred_element_type=F32)


def _mod_spec(mod, d, batch_of):
    k = mod[1]
    return pl.BlockSpec((None, 1, 1, 3 * d), lambda *g: (k, batch_of(*g), 0, 0))


def _ada_kernel(c_ref, w_ref, b_ref, o_ref):
    c = c_ref[...]
    sc = (c * jax.nn.sigmoid(c)).astype(BF16)
    o_ref[0] = _dot(sc, w_ref[0].astype(BF16)) + b_ref[0]


def _ada_call(c, ada_w, ada_b):
    n, d, d3 = ada_w.shape
    b = c.shape[0]
    tn = 1024
    return pl.pallas_call(
        _ada_kernel,
        grid=(n, d3 // tn),
        in_specs=[
            pl.BlockSpec((b, d), lambda i, j: (0, 0)),
            pl.BlockSpec((1, d, tn), lambda i, j: (i, 0, j)),
            pl.BlockSpec((1, 1, tn), lambda i, j: (i, 0, j)),
        ],
        out_specs=pl.BlockSpec((1, b, tn), lambda i, j: (i, 0, j)),
        out_shape=jax.ShapeDtypeStruct((n, b, d3), F32),
        compiler_params=_cparams(("arbitrary", "arbitrary")),
        name="ada",
    )(c, ada_w, ada_b.reshape(n, 1, d3))


def _rope_tile(t, cosv, sa, sb):
    return t * cosv + pltpu.roll(t, 32, 1) * sa + pltpu.roll(t, 96, 1) * sb


def _mla_front_kernel(x_ref, pos_ref, mod_ref, g_ref, win_ref, gq_ref, gkv_ref,
                      wq_ref, wk_ref, wv_ref, rc_ref, q_ref, k_ref, v_ref, *, d):
    x = x_ref[0]
    h = _adaln(x, g_ref[...], mod_ref[0], d).astype(BF16)
    lat = _dot(h, win_ref[...])
    cq = _rms(lat[:, 0:Q_LORA], gq_ref[...]).astype(BF16)
    ckv = _rms(lat[:, Q_LORA:Q_LORA + KV_LORA], gkv_ref[...]).astype(BF16)
    kr = lat[:, Q_LORA + KV_LORA:Q_LORA + KV_LORA + LANES]

    ang = pos_ref[0].astype(F32) * rc_ref[0:1, :]
    cosv = jnp.cos(ang)
    sinv = jnp.sin(ang)
    sa = sinv * rc_ref[1:2, :]
    sb = sinv * rc_ref[2:3, :]

    q = _dot(cq, wq_ref[...])
    for hh in range(MLA_HEADS):
        lo = hh * HEAD_PAD
        q_ref[0, :, lo:lo + LANES] = q[:, lo:lo + LANES].astype(BF16)
        q_ref[0, :, lo + LANES:lo + HEAD_PAD] = _rope_tile(
            q[:, lo + LANES:lo + HEAD_PAD], cosv, sa, sb).astype(BF16)

    kn = _dot(ckv, wk_ref[...])
    krr = _rope_tile(kr, cosv, sa, sb).astype(BF16)
    for hh in range(MLA_HEADS):
        lo = hh * HEAD_PAD
        k_ref[0, :, lo:lo + LANES] = kn[:, hh * LANES:(hh + 1) * LANES].astype(BF16)
        k_ref[0, :, lo + LANES:lo + HEAD_PAD] = krr
    v_ref[0] = _dot(ckv, wv_ref[...]).astype(BF16)


def _mla_front_call(x, pos3, mod, g, win, gq, gkv, wq, wk, wv, rc, tm):
    b, s, d = x.shape
    hq = MLA_HEADS * HEAD_PAD
    hv = MLA_HEADS * V_HEAD
    const = lambda shape: pl.BlockSpec(shape, lambda bi, si: (0,) * len(shape))
    return pl.pallas_call(
        functools.partial(_mla_front_kernel, d=d),
        grid=(b, s // tm),
        in_specs=[
            pl.BlockSpec((1, tm, d), lambda bi, si: (bi, si, 0)),
            pl.BlockSpec((1, tm, 1), lambda bi, si: (bi, si, 0)),
            _mod_spec(mod, d, lambda bi, si: bi),
            const(g.shape), const(win.shape), const(gq.shape), const(gkv.shape),
            const(wq.shape), const(wk.shape), const(wv.shape), const(rc.shape),
        ],
        out_specs=[
            pl.BlockSpec((1, tm, hq), lambda bi, si: (bi, si, 0)),
            pl.BlockSpec((1, tm, hq), lambda bi, si: (bi, si, 0)),
            pl.BlockSpec((1, tm, hv), lambda bi, si: (bi, si, 0)),
        ],
        out_shape=[
            jax.ShapeDtypeStruct((b, s, hq), BF16),
            jax.ShapeDtypeStruct((b, s, hq), BF16),
            jax.ShapeDtypeStruct((b, s, hv), BF16),
        ],
        compiler_params=_cparams(("arbitrary", "arbitrary")),
        name="mla_front",
    )(x, pos3, mod[0], g, win, gq, gkv, wq, wk, wv, rc)


def _attn_kernel(q_ref, k_ref, v_ref, o_ref, *, t, heads):
    qi = pl.program_id(2)
    tril = (lax.broadcasted_iota(jnp.int32, (t, t), 1)
            <= lax.broadcasted_iota(jnp.int32, (t, t), 0))

    def block(j, carry, masked):
        start = pl.multiple_of(j * t, t)
        out = []
        for hh in range(heads):
            m, l, acc = carry[hh]
            q = q_ref[0, :, hh * HEAD_PAD:(hh + 1) * HEAD_PAD]
            k = k_ref[0, pl.ds(start, t), hh * HEAD_PAD:(hh + 1) * HEAD_PAD]
            v = v_ref[0, pl.ds(start, t), hh * V_HEAD:(hh + 1) * V_HEAD]
            s = lax.dot_general(q, k, (((1,), (1,)), ((), ())), preferred_element_type=F32)
            if masked:
                s = jnp.where(tril, s, -jnp.inf)
            m_new = jnp.maximum(m, jnp.max(s, axis=-1, keepdims=True))
            alpha = jnp.exp(m - m_new)
            p = jnp.exp(s - m_new)
            l = alpha * l + jnp.sum(p, axis=-1, keepdims=True)
            acc = alpha * acc + _dot(p.astype(BF16), v)
            out.append((m_new, l, acc))
        return tuple(out)

    init = tuple((jnp.full((t, 1), -jnp.inf, F32), jnp.zeros((t, 1), F32),
                  jnp.zeros((t, V_HEAD), F32)) for _ in range(heads))
    carry = lax.fori_loop(0, qi, functools.partial(block, masked=False), init)
    carry = block(qi, carry, True)
    for hh in range(heads):
        _, l, acc = carry[hh]
        o_ref[0, :, hh * V_HEAD:(hh + 1) * V_HEAD] = (acc / l).astype(BF16)


def _attn_call(q, k, v, t, heads):
    b, s, _ = q.shape
    return pl.pallas_call(
        functools.partial(_attn_kernel, t=t, heads=heads),
        grid=(b, MLA_HEADS // heads, s // t),
        in_specs=[
            pl.BlockSpec((1, t, heads * HEAD_PAD), lambda bi, hi, qi: (bi, qi, hi)),
            pl.BlockSpec((1, s, heads * HEAD_PAD), lambda bi, hi, qi: (bi, 0, hi)),
            pl.BlockSpec((1, s, heads * V_HEAD), lambda bi, hi, qi: (bi, 0, hi)),
        ],
        out_specs=pl.BlockSpec((1, t, heads * V_HEAD), lambda bi, hi, qi: (bi, qi, hi)),
        out_shape=jax.ShapeDtypeStruct((b, s, MLA_HEADS * V_HEAD), BF16),
        compiler_params=_cparams(("arbitrary", "arbitrary", "arbitrary")),
        name="attention",
    )(q, k, v)


def _attn_out_kernel(x_ref, o_ref, w_ref, moda_ref, modb_ref, g_ref, x1_ref, h_ref, *, d):
    y = _dot(o_ref[0], w_ref[...])
    x1 = x_ref[0] + moda_ref[0][:, 2 * d:3 * d] * y
    x1_ref[0] = x1
    h_ref[0] = _adaln(x1, g_ref[...], modb_ref[0], d).astype(BF16)


def _attn_out_call(x, o, w, moda, modb, g, tm):
    b, s, d = x.shape
    tile = lambda: pl.BlockSpec((1, tm, d), lambda bi, si: (bi, si, 0))
    modspec = lambda mod: _mod_spec(mod, d, lambda bi, si: bi)
    return pl.pallas_call(
        functools.partial(_attn_out_kernel, d=d),
        grid=(b, s // tm),
        in_specs=[tile(), tile(), pl.BlockSpec(w.shape, lambda bi, si: (0, 0)),
                  modspec(moda), modspec(modb), pl.BlockSpec(g.shape, lambda bi, si: (0, 0))],
        out_specs=[tile(), tile()],
        out_shape=[jax.ShapeDtypeStruct((b, s, d), F32), jax.ShapeDtypeStruct((b, s, d), BF16)],
        compiler_params=_cparams(("arbitrary", "arbitrary")),
        name="attn_out",
    )(x, o, w, moda[0], modb[0], g)


def _ffn_body(valid, h_bf, wg_ref, wu_ref, wd_ref, o_ref, tm, sub, extra=None):
    n_sub = tm // sub
    for n in range(1, n_sub + 1):
        lo = (n - 1) * sub
        cond = valid > lo if n == n_sub else jnp.logical_and(valid > lo, valid <= n * sub)

        @pl.when(cond)
        def _():
            if extra is not None:
                extra()
            rows = slice(0, n * sub)
            h = h_bf[rows, :]
            g = _dot(h, wg_ref[0].astype(BF16))
            u = _dot(h, wu_ref[0].astype(BF16))
            a = (g * jax.nn.sigmoid(g) * u).astype(BF16)
            o_ref[rows, :] += _dot(a, wd_ref[0].astype(BF16))


def _ffn_dense_kernel(te_ref, tv_ref, h_ref, wg_ref, wu_ref, wd_ref, o_ref, *, tm, sub):
    i = pl.program_id(0)
    j = pl.program_id(1)

    @pl.when(j == 0)
    def _():
        o_ref[...] = jnp.zeros_like(o_ref)

    _ffn_body(tv_ref[i], h_ref, wg_ref, wu_ref, wd_ref, o_ref, tm, sub)


def _ffn_gather_kernel(te_ref, tv_ref, src_ref, srcn_ref, h_hbm, wg_ref, wu_ref, wd_ref, o_ref,
                       hf_ref, hb_ref, sem, *, tm, sub, rows_per_step):
    i = pl.program_id(0)
    j = pl.program_id(1)
    valid = tv_ref[i]
    n_rows = hf_ref.shape[0]

    def row_copy(idx_ref, r):
        return pltpu.make_async_copy(h_hbm.at[pl.ds(idx_ref[0, 0, r], 1), :],
                                     hf_ref.at[pl.ds(r, 1), :], sem)

    def wait_rows():
        pltpu.make_async_copy(h_hbm.at[pl.ds(0, n_rows), :], hf_ref, sem).wait()

    @pl.when(j == 0)
    def _():
        o_ref[...] = jnp.zeros_like(o_ref)

        @pl.when(jnp.logical_and(i == 0, valid > 0))
        def _():
            def issue(r, c):
                row_copy(src_ref, r).start()
                return c
            lax.fori_loop(0, n_rows, issue, 0)
            wait_rows()

        @pl.when(jnp.logical_and(i > 0, tv_ref[jnp.maximum(i - 1, 0)] > 0))
        def _():
            wait_rows()

        @pl.when(valid > 0)
        def _():
            hb_ref[...] = hf_ref[0:tm, :].astype(BF16)

    def request_next_rows():
        for q in range(rows_per_step):
            row_copy(srcn_ref, j * rows_per_step + q).start()

    _ffn_body(valid, hb_ref, wg_ref, wu_ref, wd_ref, o_ref, tm, sub, request_next_rows)


def _ffn_call(h, wg, wu, wd, tile_expert, tile_valid, src, tm, sub, tf):
    d = wg.shape[1]
    f = wg.shape[2]
    nf = f // tf
    nt = tile_expert.shape[0]

    def jj(i, j, tv):
        return jnp.where(tv[i] > 0, j, nf - 1)

    w_specs = [
        pl.BlockSpec((1, d, tf), lambda i, j, te, tv: (te[i], 0, jj(i, j, tv))),
        pl.BlockSpec((1, d, tf), lambda i, j, te, tv: (te[i], 0, jj(i, j, tv))),
        pl.BlockSpec((1, tf, d), lambda i, j, te, tv: (te[i], jj(i, j, tv), 0)),
    ]
    out_spec = pl.BlockSpec((tm, d), lambda i, j, te, tv: (i, 0))
    out_shape = jax.ShapeDtypeStruct((nt * tm, d), F32)
    if src is None:
        grid_spec = pltpu.PrefetchScalarGridSpec(
            num_scalar_prefetch=2, grid=(nt, nf),
            in_specs=[pl.BlockSpec((tm, d), lambda i, j, te, tv: (i, 0))] + w_specs,
            out_specs=out_spec)
        return pl.pallas_call(
            functools.partial(_ffn_dense_kernel, tm=tm, sub=sub), grid_spec=grid_spec,
            out_shape=out_shape,
            compiler_params=_cparams(("arbitrary", "arbitrary")), name="ffn_dense",
        )(tile_expert, tile_valid, h, wg, wu, wd)
    n_rows = src.shape[2]
    assert n_rows % nf == 0 and n_rows >= tm
    grid_spec = pltpu.PrefetchScalarGridSpec(
        num_scalar_prefetch=2, grid=(nt, nf),
        in_specs=[pl.BlockSpec((1, 1, n_rows), lambda i, j, te, tv: (i, 0, 0),
                               memory_space=pltpu.SMEM),
                  pl.BlockSpec((1, 1, n_rows),
                               lambda i, j, te, tv: (jnp.minimum(i + 1, nt - 1), 0, 0),
                               memory_space=pltpu.SMEM),
                  pl.BlockSpec(memory_space=pl.ANY)] + w_specs,
        out_specs=out_spec,
        scratch_shapes=[pltpu.VMEM((n_rows, d), F32), pltpu.VMEM((tm, d), BF16),
                        pltpu.SemaphoreType.DMA(())])
    return pl.pallas_call(
        functools.partial(_ffn_gather_kernel, tm=tm, sub=sub, rows_per_step=n_rows // nf),
        grid_spec=grid_spec, out_shape=out_shape,
        compiler_params=_cparams(("arbitrary", "arbitrary"), VMEM_LIMIT_MOE), name="ffn_moe",
    )(tile_expert, tile_valid, src, src, h, wg, wu, wd)


def _conv_front_kernel(x1_ref, y_ref, moda_ref, modb_ref, g_ref, w_ref, b_ref,
                       x2_ref, glu_ref, *, d):
    x2 = x1_ref[0] + moda_ref[0][:, 2 * d:3 * d] * y_ref[0]
    x2_ref[0] = x2
    h = _adaln(x2, g_ref[...], modb_ref[0], d).astype(BF16)
    u = _dot(h, w_ref[...]) + b_ref[...]
    glu_ref[0] = u[:, 0:d] * jax.nn.sigmoid(u[:, d:2 * d])


def _conv_front_call(x1, y, moda, modb, g, w, bias, tm):
    b, s, d = x1.shape
    tile = lambda: pl.BlockSpec((1, tm, d), lambda bi, si: (bi, si, 0))
    modspec = lambda mod: _mod_spec(mod, d, lambda bi, si: bi)
    const = lambda a: pl.BlockSpec(a.shape, lambda bi, si: (0,) * a.ndim)
    return pl.pallas_call(
        functools.partial(_conv_front_kernel, d=d),
        grid=(b, s // tm),
        in_specs=[tile(), tile(), modspec(moda), modspec(modb), const(g), const(w),
                  const(bias)],
        out_specs=[tile(), tile()],
        out_shape=[jax.ShapeDtypeStruct((b, s, d), F32), jax.ShapeDtypeStruct((b, s, d), F32)],
        compiler_params=_cparams(("arbitrary", "arbitrary")),
        name="conv_front",
    )(x1, y, moda[0], modb[0], g, w, bias)


def _conv_back_kernel(glu_ref, halo_ref, x2_ref, moda_ref, modb_ref, wdw_ref, bdw_ref,
                      lng_ref, lnb_ref, w2_ref, b2_ref, g_ref, wrh_ref, wrl_ref,
                      x3_ref, h4_ref, route_ref, ext_ref, conv_ref, base_ref, sh_ref,
                      *, d, tm, rows, cols):
    bi = pl.program_id(0)
    si = pl.program_id(1)

    @pl.when(jnp.logical_and(bi == 0, si == 0))
    def _():
        base_ref[...] = jnp.zeros_like(base_ref)

    ext_ref[0:HALO, :] = jnp.where(si == 0, 0.0, halo_ref[0])
    ext_ref[HALO:HALO + tm, :] = glu_ref[0]

    off = HALO - (CONV_WIDTH - 1)
    span = sh_ref.shape[1]
    for c0 in range(0, d, cols):
        for r in range(1, SUBLANES):
            sh_ref[r - 1] = ext_ref[r:r + span, c0:c0 + cols]
        for r0 in range(0, tm, rows):
            groups = (rows // SUBLANES, SUBLANES, cols)
            acc = jnp.zeros(groups, F32) + bdw_ref[:, c0:c0 + cols][None]
            for k in range(CONV_WIDTH):
                r = (off + k) % SUBLANES
                a = r0 + off + k - r
                if r == 0:
                    tap = ext_ref[a:a + rows, c0:c0 + cols]
                else:
                    tap = sh_ref[r - 1, a:a + rows, :]
                acc = acc + wdw_ref[k, :, c0:c0 + cols][None] * tap.reshape(groups)
            conv_ref[r0:r0 + rows, c0:c0 + cols] = acc.reshape(rows, cols)

    u = conv_ref[...]
    mu = jnp.mean(u, axis=-1, keepdims=True)
    uc = u - mu
    var = jnp.mean(uc * uc, axis=-1, keepdims=True)
    u = uc * lax.rsqrt(var + EPS) * lng_ref[...] + lnb_ref[...]
    u = (u * jax.nn.sigmoid(u)).astype(BF16)
    y = _dot(u, w2_ref[...]) + b2_ref[...]
    x3 = x2_ref[0] + moda_ref[0][:, 2 * d:3 * d] * y
    x3_ref[0] = x3
    h4 = _adaln(x3, g_ref[...], modb_ref[0], d)
    h4_ref[0] = h4

    hh = h4.astype(BF16)
    hl = (h4 - hh.astype(F32)).astype(BF16)
    logits = _dot(hh, wrh_ref[...]) + _dot(hl, wrh_ref[...]) + _dot(hh, wrl_ref[...])
    lane = lax.broadcasted_iota(jnp.int32, (tm, LANES), 1).astype(F32)
    lg = jnp.where(lane < N_EXPERTS, logits, -jnp.inf)
    m1 = jnp.max(lg, axis=-1, keepdims=True)
    i1 = jnp.min(jnp.where(lg == m1, lane, float(LANES)), axis=-1, keepdims=True)
    lg2 = jnp.where(lane == i1, -jnp.inf, lg)
    m2 = jnp.max(lg2, axis=-1, keepdims=True)
    i2 = jnp.min(jnp.where(lg2 == m2, lane, float(LANES)), axis=-1, keepdims=True)
    e2 = jnp.exp(m2 - m1)
    w0 = 1.0 / (1.0 + e2)
    w1 = e2 / (1.0 + e2)
    oh0 = (lane == i1).astype(F32)
    oh1 = (lane == i2).astype(F32)
    both = oh0 + oh1
    tri = (lax.broadcasted_iota(jnp.int32, (tm, tm), 0)
           > lax.broadcasted_iota(jnp.int32, (tm, tm), 1)).astype(BF16)
    cnt = _dot(tri, both.astype(BF16)) + base_ref[...]
    r0 = jnp.sum(cnt * oh0, axis=-1, keepdims=True)
    r1 = jnp.sum(cnt * oh1, axis=-1, keepdims=True)
    base_ref[...] = base_ref[...] + jnp.sum(both, axis=0, keepdims=True)
    route = jnp.where(lane == 0.0, i1, 0.0)
    route = jnp.where(lane == 1.0, i2, route)
    route = jnp.where(lane == 2.0, w0, route)
    route = jnp.where(lane == 3.0, w1, route)
    route = jnp.where(lane == 4.0, r0, route)
    route = jnp.where(lane == 5.0, r1, route)
    route_ref[0] = route


def _conv_back_call(glu, x2, moda, modb, wdw, bdw, lng, lnb, w2, b2, g, wrh, wrl, tm):
    b, s, d = x2.shape
    tile = lambda: pl.BlockSpec((1, tm, d), lambda bi, si: (bi, si, 0))
    modspec = lambda mod: _mod_spec(mod, d, lambda bi, si: bi)
    const = lambda a: pl.BlockSpec(a.shape, lambda bi, si: (0,) * a.ndim)
    hpt = tm // HALO
    halo = pl.BlockSpec((1, HALO, d), lambda bi, si: (bi, jnp.maximum(si * hpt - 1, 0), 0))
    cols = 2 * LANES
    return pl.pallas_call(
        functools.partial(_conv_back_kernel, d=d, tm=tm, rows=64, cols=cols),
        grid=(b, s // tm),
        in_specs=[tile(), halo, tile(), modspec(moda), modspec(modb), const(wdw), const(bdw),
                  const(lng), const(lnb), const(w2), const(b2), const(g), const(wrh), const(wrl)],
        out_specs=[tile(), tile(), pl.BlockSpec((1, tm, LANES), lambda bi, si: (bi, si, 0))],
        out_shape=[jax.ShapeDtypeStruct((b, s, d), F32), jax.ShapeDtypeStruct((b, s, d), F32),
                   jax.ShapeDtypeStruct((b, s, LANES), F32)],
        scratch_shapes=[pltpu.VMEM((tm + HALO, d), F32), pltpu.VMEM((tm, d), F32),
                        pltpu.VMEM((1, LANES), F32),
                        pltpu.VMEM((SUBLANES - 1, tm + HALO - SUBLANES, cols), F32)],
        compiler_params=_cparams(("arbitrary", "arbitrary")),
        name="conv_back",
    )(glu, glu, x2, moda[0], modb[0], wdw, bdw, lng, lnb, w2, b2, g, wrh, wrl)


def _combine_kernel(p0_ref, p1_ref, p0n_ref, p1n_ref, x_ref, route_ref, mod_ref, fg_ref, ys_hbm,
                    o_ref, y_ref, sem, *, d, tm):
    i = pl.program_id(0)
    slot = i % 2

    def request(pa_ref, pb_ref, sl):
        def issue(r, c):
            pltpu.make_async_copy(ys_hbm.at[pl.ds(pa_ref[0, 0, r], 1), :],
                                  y_ref.at[sl, 0, pl.ds(r, 1), :], sem.at[sl, 0]).start()
            pltpu.make_async_copy(ys_hbm.at[pl.ds(pb_ref[0, 0, r], 1), :],
                                  y_ref.at[sl, 1, pl.ds(r, 1), :], sem.at[sl, 1]).start()
            return c
        lax.fori_loop(0, tm, issue, 0, unroll=8)

    @pl.when(i == 0)
    def _():
        request(p0_ref, p1_ref, 0)

    @pl.when(i + 1 < pl.num_programs(0))
    def _():
        request(p0n_ref, p1n_ref, 1 - slot)

    for pick in range(2):
        pltpu.make_async_copy(ys_hbm.at[pl.ds(0, tm), :], y_ref.at[slot, pick],
                              sem.at[slot, pick]).wait()
    w0 = route_ref[:, 2:3]
    w1 = route_ref[:, 3:4]
    y = w0 * y_ref[slot, 0] + w1 * y_ref[slot, 1]
    x = x_ref[...] + mod_ref[0][:, 2 * d:3 * d] * y
    o_ref[...] = _rms(x, fg_ref[...])


def _combine_call(p0, p1, x3, route, mod, fg, ys, tm, tiles_per_batch):
    t, d = x3.shape
    nt = t // tm
    nxt = lambda i: (jnp.minimum(i + 1, nt - 1), 0, 0)
    return pl.pallas_call(
        functools.partial(_combine_kernel, d=d, tm=tm),
        grid=(nt,),
        in_specs=[
            pl.BlockSpec((1, 1, tm), lambda i: (i, 0, 0), memory_space=pltpu.SMEM),
            pl.BlockSpec((1, 1, tm), lambda i: (i, 0, 0), memory_space=pltpu.SMEM),
            pl.BlockSpec((1, 1, tm), nxt, memory_space=pltpu.SMEM),
            pl.BlockSpec((1, 1, tm), nxt, memory_space=pltpu.SMEM),
            pl.BlockSpec((tm, d), lambda i: (i, 0)),
            pl.BlockSpec((tm, LANES), lambda i: (i, 0)),
            _mod_spec(mod, d, lambda i: i // tiles_per_batch),
            pl.BlockSpec(fg.shape, lambda i: (0, 0)),
            pl.BlockSpec(memory_space=pl.ANY),
        ],
        out_specs=pl.BlockSpec((tm, d), lambda i: (i, 0)),
        out_shape=jax.ShapeDtypeStruct((t, d), F32),
        scratch_shapes=[pltpu.VMEM((2, 2, tm, d), F32), pltpu.SemaphoreType.DMA((2, 2))],
        compiler_params=_cparams(("arbitrary",)),
        name="combine",
    )(p0, p1, p0, p1, x3, route, mod[0], fg, ys)


def _rope_consts():
    lane = np.arange(LANES)
    half = QK_ROPE // 2
    freq = np.where(lane < QK_ROPE, ROPE_THETA ** (-(2.0 * (lane % half)) / QK_ROPE), 0.0)
    sa = np.where((lane >= half) & (lane < QK_ROPE), 1.0, 0.0)
    sb = np.where(lane < half, -1.0, 0.0)
    rc = np.zeros((8, LANES), np.float32)
    rc[0], rc[1], rc[2] = freq, sa, sb
    return jnp.asarray(rc)


def _moe_plan(route, tm, n_tiles, n_rows):
    t = route.shape[0]
    assert n_tiles == (2 * t) // tm + N_EXPERTS and (2 * t) % tm == 0
    e0 = route[:, 0].astype(jnp.int32)
    e1 = route[:, 1].astype(jnp.int32)
    r0 = route[:, 4].astype(jnp.int32)
    r1 = route[:, 5].astype(jnp.int32)
    ids = jnp.arange(N_EXPERTS, dtype=jnp.int32)
    counts = (jnp.sum(e0[:, None] == ids[None, :], axis=0)
              + jnp.sum(e1[:, None] == ids[None, :], axis=0)).astype(jnp.int32)
    tiles = (counts + tm - 1) // tm
    tile_end = jnp.cumsum(tiles)
    tile_start = tile_end - tiles
    pos0 = tile_start[e0] * tm + r0
    pos1 = tile_start[e1] * tm + r1
    tok = jnp.arange(t, dtype=jnp.int32)
    pos = jnp.concatenate([pos0, pos1])
    src = jnp.zeros((n_tiles * n_rows,), jnp.int32).at[(pos // tm) * n_rows + pos % tm].set(
        jnp.concatenate([tok, tok]), unique_indices=True, mode="promise_in_bounds")
    ti = jnp.arange(n_tiles, dtype=jnp.int32)
    used = tile_end[-1]
    te = jnp.sum(tile_end[None, :] <= jnp.minimum(ti, used - 1)[:, None], axis=1)
    te = jnp.minimum(te, N_EXPERTS - 1).astype(jnp.int32)
    tv = jnp.clip(counts[te] - (ti - tile_start[te]) * tm, 0, tm)
    tv = jnp.where(ti < used, tv, 0).astype(jnp.int32)
    return pos0, pos1, src.reshape(n_tiles, 1, n_rows), te, tv


def kernel(x, c, positions, ada_w, ada_b, norm_g, mla_w_in, mla_g_q, mla_g_kv, mla_w_uq,
           mla_w_ukv, mla_w_out, conv_w_pw1, conv_b_pw1, conv_w_dw, conv_b_dw, conv_ln_g,
           conv_ln_b, conv_w_pw2, conv_b_pw2, ffn_w_gate, ffn_w_up, ffn_w_down, moe_w_router,
           moe_w_gate, moe_w_up, moe_w_down, final_g):
    b, s, d = x.shape
    t = b * s
    f = ffn_w_gate.shape[-1]
    assert ada_w.shape[0] == 2 and mla_w_in.shape[0] == 1 and conv_w_pw1.shape[0] == 1
    assert moe_w_gate.shape[1] == N_EXPERTS and conv_w_dw.shape[1] == CONV_WIDTH

    scale = (QK_NOPE + QK_ROPE) ** -0.5
    w_in = jnp.pad(mla_w_in[0], ((0, 0), (0, LANES - QK_ROPE))).astype(BF16)
    wq = mla_w_uq[0].reshape(Q_LORA, MLA_HEADS, QK_NOPE + QK_ROPE) * scale
    wq = jnp.pad(wq, ((0, 0), (0, 0), (0, HEAD_PAD - QK_NOPE - QK_ROPE)))
    wq = wq.reshape(Q_LORA, MLA_HEADS * HEAD_PAD).astype(BF16)
    wkv = mla_w_ukv[0].reshape(KV_LORA, MLA_HEADS, QK_NOPE + V_HEAD)
    wk = wkv[:, :, :QK_NOPE].reshape(KV_LORA, MLA_HEADS * QK_NOPE).astype(BF16)
    wv = wkv[:, :, QK_NOPE:].reshape(KV_LORA, MLA_HEADS * V_HEAD).astype(BF16)
    w_out = mla_w_out[0].astype(BF16)
    w_pw1 = conv_w_pw1[0].astype(BF16)
    w_pw2 = conv_w_pw2[0].astype(BF16)
    wr = jnp.pad(moe_w_router[0], ((0, 0), (0, LANES - N_EXPERTS)))
    wr_hi = wr.astype(BF16)
    wr_lo = (wr - wr_hi.astype(F32)).astype(BF16)
    row = lambda v: v.reshape(1, -1)

    mods = _ada_call(c, ada_w.reshape(4, d, 3 * d), ada_b.reshape(4, 3 * d))
    mods = mods.reshape(4, b, 1, 3 * d)
    mod = [(mods, i) for i in range(4)]

    q, k, v = _mla_front_call(x, positions.reshape(b, s, 1), mod[0], row(norm_g[0, 0]), w_in,
                              row(mla_g_q[0]), row(mla_g_kv[0]), wq, wk, wv, _rope_consts(),
                              tm=TILES["mla_front"])
    o = _attn_call(q, k, v, t=TILES["attn_block"], heads=TILES["attn_heads"])
    x1, h2 = _attn_out_call(x, o, w_out, mod[0], mod[1], row(norm_g[0, 1]),
                            tm=TILES["attn_out"])
    tm_d = TILES["ffn_rows"]
    nt_d = t // tm_d
    y1 = _ffn_call(h2.reshape(t, d), ffn_w_gate.astype(BF16), ffn_w_up.astype(BF16),
                   ffn_w_down.astype(BF16), jnp.zeros((nt_d,), jnp.int32),
                   jnp.full((nt_d,), tm_d, jnp.int32), None, tm=tm_d, sub=tm_d,
                   tf=TILES["ffn_cols"])

    x2, glu = _conv_front_call(x1, y1.reshape(b, s, d), mod[1], mod[2], row(norm_g[1, 0]),
                               w_pw1, row(conv_b_pw1[0]), tm=TILES["conv"])
    w_dw8 = jnp.broadcast_to(conv_w_dw[0][:, None, :], (CONV_WIDTH, SUBLANES, d))
    b_dw8 = jnp.broadcast_to(row(conv_b_dw[0]), (SUBLANES, d))
    x3, h4, route = _conv_back_call(glu, x2, mod[2], mod[3], w_dw8, b_dw8,
                                    row(conv_ln_g[0]), row(conv_ln_b[0]), w_pw2,
                                    row(conv_b_pw2[0]), row(norm_g[1, 1]), wr_hi, wr_lo,
                                    tm=TILES["conv"])
    route = route.reshape(t, LANES)
    tm_e = TILES["ffn_rows"]
    tf_e = TILES["ffn_cols"]
    nt_e = (2 * t) // tm_e + N_EXPERTS
    nf_e = f // tf_e
    rps = -(-tm_e // nf_e)
    while (rps * nf_e) % SUBLANES:
        rps += 1
    rows_e = rps * nf_e
    pos0, pos1, src, te, tv = _moe_plan(route, tm_e, nt_e, rows_e)
    ys = _ffn_call(h4.reshape(t, d), moe_w_gate[0], moe_w_up[0], moe_w_down[0], te, tv, src,
                   tm=tm_e, sub=TILES["moe_row_class"], tf=tf_e)
    tm_c = TILES["combine"]
    out = _combine_call(pos0.reshape(t // tm_c, 1, tm_c), pos1.reshape(t // tm_c, 1, tm_c),
                        x3.reshape(t, d), route, mod[3], row(final_g), ys, tm=tm_c,
                        tiles_per_batch=s // tm_c)
    return out.reshape(b, s, d)
```

```python
import functools

import numpy as np
import jax
import jax.numpy as jnp
from jax import lax
from jax.experimental import pallas as pl
from jax.experimental.pallas import tpu as pltpu

F32 = jnp.float32
BF16 = jnp.bfloat16

MLA_HEADS = 16
Q_LORA = 512
KV_LORA = 512
QK_NOPE = 128
QK_ROPE = 64
V_HEAD = 128
ROPE_THETA = 10000.0
CONV_WIDTH = 31
N_EXPERTS = 8
EPS = 1e-6

LANES = 128
SUBLANES = 8
HEAD_PAD = 2 * LANES
HALO = 32
VMEM_LIMIT = 56 * 1024 * 1024


def _cparams(sem):
    return pltpu.CompilerParams(dimension_semantics=sem, vmem_limit_bytes=VMEM_LIMIT)


def _rms(x, g):
    return x * lax.rsqrt(jnp.mean(x * x, axis=-1, keepdims=True) + EPS) * g


def _adaln(x, g, mod, d):
    return _rms(x, g) * (1.0 + mod[:, d:2 * d]) + mod[:, 0:d]


def _dot(a, b):
    return jnp.dot(a, b, preferred_element_type=F32)


def _mod_spec(mod, d, batch_of):
    k = mod[1]
    return pl.BlockSpec((None, 1, 1, 3 * d), lambda *g: (k, batch_of(*g), 0, 0))


def _ada_kernel(c_ref, w_ref, b_ref, o_ref):
    c = c_ref[...]
    sc = (c * jax.nn.sigmoid(c)).astype(BF16)
    o_ref[0] = _dot(sc, w_ref[0].astype(BF16)) + b_ref[0]


def _ada_call(c, ada_w, ada_b):
    n, d, d3 = ada_w.shape
    b = c.shape[0]
    tn = 1024
    return pl.pallas_call(
        _ada_kernel,
        grid=(n, d3 // tn),
        in_specs=[
            pl.BlockSpec((b, d), lambda i, j: (0, 0)),
            pl.BlockSpec((1, d, tn), lambda i, j: (i, 0, j)),
            pl.BlockSpec((1, 1, tn), lambda i, j: (i, 0, j)),
        ],
        out_specs=pl.BlockSpec((1, b, tn), lambda i, j: (i, 0, j)),
        out_shape=jax.ShapeDtypeStruct((n, b, d3), F32),
        compiler_params=_cparams(("arbitrary", "arbitrary")),
        name="ada",
    )(c, ada_w, ada_b.reshape(n, 1, d3))


def _rope_tile(t, cosv, sa, sb):
    return t * cosv + pltpu.roll(t, 32, 1) * sa + pltpu.roll(t, 96, 1) * sb


def _mla_front_kernel(x_ref, pos_ref, mod_ref, g_ref, win_ref, gq_ref, gkv_ref,
                      wq_ref, wk_ref, wv_ref, rc_ref, q_ref, k_ref, v_ref, *, d):
    x = x_ref[0]
    h = _adaln(x, g_ref[...], mod_ref[0], d).astype(BF16)
    lat = _dot(h, win_ref[...])
    cq = _rms(lat[:, 0:Q_LORA], gq_ref[...]).astype(BF16)
    ckv = _rms(lat[:, Q_LORA:Q_LORA + KV_LORA], gkv_ref[...]).astype(BF16)
    kr = lat[:, Q_LORA + KV_LORA:Q_LORA + KV_LORA + LANES]

    ang = pos_ref[0].astype(F32) * rc_ref[0:1, :]
    cosv = jnp.cos(ang)
    sinv = jnp.sin(ang)
    sa = sinv * rc_ref[1:2, :]
    sb = sinv * rc_ref[2:3, :]

    q = _dot(cq, wq_ref[...])
    for hh in range(MLA_HEADS):
        lo = hh * HEAD_PAD
        q_ref[0, :, lo:lo + LANES] = q[:, lo:lo + LANES].astype(BF16)
        q_ref[0, :, lo + LANES:lo + HEAD_PAD] = _rope_tile(
            q[:, lo + LANES:lo + HEAD_PAD], cosv, sa, sb).astype(BF16)

    kn = _dot(ckv, wk_ref[...])
    krr = _rope_tile(kr, cosv, sa, sb).astype(BF16)
    for hh in range(MLA_HEADS):
        lo = hh * HEAD_PAD
        k_ref[0, :, lo:lo + LANES] = kn[:, hh * LANES:(hh + 1) * LANES].astype(BF16)
        k_ref[0, :, lo + LANES:lo + HEAD_PAD] = krr
    v_ref[0] = _dot(ckv, wv_ref[...]).astype(BF16)


def _mla_front_call(x, pos3, mod, g, win, gq, gkv, wq, wk, wv, rc, tm):
    b, s, d = x.shape
    hq = MLA_HEADS * HEAD_PAD
    hv = MLA_HEADS * V_HEAD
    const = lambda shape: pl.BlockSpec(shape, lambda bi, si: (0,) * len(shape))
    return pl.pallas_call(
        functools.partial(_mla_front_kernel, d=d),
        grid=(b, s // tm),
        in_specs=[
            pl.BlockSpec((1, tm, d), lambda bi, si: (bi, si, 0)),
            pl.BlockSpec((1, tm, 1), lambda bi, si: (bi, si, 0)),
            _mod_spec(mod, d, lambda bi, si: bi),
            const(g.shape), const(win.shape), const(gq.shape), const(gkv.shape),
            const(wq.shape), const(wk.shape), const(wv.shape), const(rc.shape),
        ],
        out_specs=[
            pl.BlockSpec((1, tm, hq), lambda bi, si: (bi, si, 0)),
            pl.BlockSpec((1, tm, hq), lambda bi, si: (bi, si, 0)),
            pl.BlockSpec((1, tm, hv), lambda bi, si: (bi, si, 0)),
        ],
        out_shape=[
            jax.ShapeDtypeStruct((b, s, hq), BF16),
            jax.ShapeDtypeStruct((b, s, hq), BF16),
            jax.ShapeDtypeStruct((b, s, hv), BF16),
        ],
        compiler_params=_cparams(("arbitrary", "arbitrary")),
        name="mla_front",
    )(x, pos3, mod[0], g, win, gq, gkv, wq, wk, wv, rc)


def _attn_kernel(q_ref, k_ref, v_ref, o_ref, *, t, heads):
    qi = pl.program_id(2)
    tril = (lax.broadcasted_iota(jnp.int32, (t, t), 1)
            <= lax.broadcasted_iota(jnp.int32, (t, t), 0))

    def block(j, carry, masked):
        start = pl.multiple_of(j * t, t)
        out = []
        for hh in range(heads):
            m, l, acc = carry[hh]
            q = q_ref[0, :, hh * HEAD_PAD:(hh + 1) * HEAD_PAD]
            k = k_ref[0, pl.ds(start, t), hh * HEAD_PAD:(hh + 1) * HEAD_PAD]
            v = v_ref[0, pl.ds(start, t), hh * V_HEAD:(hh + 1) * V_HEAD]
            s = lax.dot_general(q, k, (((1,), (1,)), ((), ())), preferred_element_type=F32)
            if masked:
                s = jnp.where(tril, s, -jnp.inf)
            m_new = jnp.maximum(m, jnp.max(s, axis=-1, keepdims=True))
            alpha = jnp.exp(m - m_new)
            p = jnp.exp(s - m_new)
            l = alpha * l + jnp.sum(p, axis=-1, keepdims=True)
            acc = alpha * acc + _dot(p.astype(BF16), v)
            out.append((m_new, l, acc))
        return tuple(out)

    init = tuple((jnp.full((t, 1), -jnp.inf, F32), jnp.zeros((t, 1), F32),
                  jnp.zeros((t, V_HEAD), F32)) for _ in range(heads))
    carry = lax.fori_loop(0, qi, functools.partial(block, masked=False), init)
    carry = block(qi, carry, True)
    for hh in range(heads):
        _, l, acc = carry[hh]
        o_ref[0, :, hh * V_HEAD:(hh + 1) * V_HEAD] = (acc / l).astype(BF16)


def _attn_call(q, k, v, t, heads):
    b, s, _ = q.shape
    return pl.pallas_call(
        functools.partial(_attn_kernel, t=t, heads=heads),
        grid=(b, MLA_HEADS // heads, s // t),
        in_specs=[
            pl.BlockSpec((1, t, heads * HEAD_PAD), lambda bi, hi, qi: (bi, qi, hi)),
            pl.BlockSpec((1, s, heads * HEAD_PAD), lambda bi, hi, qi: (bi, 0, hi)),
            pl.BlockSpec((1, s, heads * V_HEAD), lambda bi, hi, qi: (bi, 0, hi)),
        ],
        out_specs=pl.BlockSpec((1, t, heads * V_HEAD), lambda bi, hi, qi: (bi, qi, hi)),
        out_shape=jax.ShapeDtypeStruct((b, s, MLA_HEADS * V_HEAD), BF16),
        compiler_params=_cparams(("arbitrary", "arbitrary", "arbitrary")),
        name="attention",
    )(q, k, v)


def _attn_out_kernel(x_ref, o_ref, w_ref, moda_ref, modb_ref, g_ref, x1_ref, h_ref, *, d):
    y = _dot(o_ref[0], w_ref[...])
    x1 = x_ref[0] + moda_ref[0][:, 2 * d:3 * d] * y
    x1_ref[0] = x1
    h_ref[0] = _adaln(x1, g_ref[...], modb_ref[0], d).astype(BF16)


def _attn_out_call(x, o, w, moda, modb, g, tm):
    b, s, d = x.shape
    tile = lambda: pl.BlockSpec((1, tm, d), lambda bi, si: (bi, si, 0))
    modspec = lambda mod: _mod_spec(mod, d, lambda bi, si: bi)
    return pl.pallas_call(
        functools.partial(_attn_out_kernel, d=d),
        grid=(b, s // tm),
        in_specs=[tile(), tile(), pl.BlockSpec(w.shape, lambda bi, si: (0, 0)),
                  modspec(moda), modspec(modb), pl.BlockSpec(g.shape, lambda bi, si: (0, 0))],
        out_specs=[tile(), tile()],
        out_shape=[jax.ShapeDtypeStruct((b, s, d), F32), jax.ShapeDtypeStruct((b, s, d), BF16)],
        compiler_params=_cparams(("arbitrary", "arbitrary")),
        name="attn_out",
    )(x, o, w, moda[0], modb[0], g)


def _ffn_body(valid, h_bf, wg_ref, wu_ref, wd_ref, o_ref, tm, sub, extra=None):
    n_sub = tm // sub
    for n in range(1, n_sub + 1):
        lo = (n - 1) * sub
        cond = valid > lo if n == n_sub else jnp.logical_and(valid > lo, valid <= n * sub)

        @pl.when(cond)
        def _():
            if extra is not None:
                extra()
            rows = slice(0, n * sub)
            h = h_bf[rows, :]
            g = _dot(h, wg_ref[0].astype(BF16))
            u = _dot(h, wu_ref[0].astype(BF16))
            a = (g * jax.nn.sigmoid(g) * u).astype(BF16)
            o_ref[rows, :] += _dot(a, wd_ref[0].astype(BF16))


def _ffn_dense_kernel(te_ref, tv_ref, h_ref, wg_ref, wu_ref, wd_ref, o_ref, *, tm, sub):
    i = pl.program_id(0)
    j = pl.program_id(1)

    @pl.when(j == 0)
    def _():
        o_ref[...] = jnp.zeros_like(o_ref)

    _ffn_body(tv_ref[i], h_ref, wg_ref, wu_ref, wd_ref, o_ref, tm, sub)


def _ffn_gather_kernel(te_ref, tv_ref, src_ref, srcn_ref, h_hbm, wg_ref, wu_ref, wd_ref, o_ref,
                       hf_ref, hb_ref, sem, *, tm, sub, rows_per_step):
    i = pl.program_id(0)
    j = pl.program_id(1)
    valid = tv_ref[i]
    n_rows = hf_ref.shape[0]

    def row_copy(idx_ref, r):
        return pltpu.make_async_copy(h_hbm.at[pl.ds(idx_ref[0, 0, r], 1), :],
                                     hf_ref.at[pl.ds(r, 1), :], sem)

    def wait_rows():
        pltpu.make_async_copy(h_hbm.at[pl.ds(0, n_rows), :], hf_ref, sem).wait()

    @pl.when(j == 0)
    def _():
        o_ref[...] = jnp.zeros_like(o_ref)

        @pl.when(jnp.logical_and(i == 0, valid > 0))
        def _():
            def issue(r, c):
                row_copy(src_ref, r).start()
                return c
            lax.fori_loop(0, n_rows, issue, 0)
            wait_rows()

        @pl.when(jnp.logical_and(i > 0, tv_ref[jnp.maximum(i - 1, 0)] > 0))
        def _():
            wait_rows()

        @pl.when(valid > 0)
        def _():
            hb_ref[...] = hf_ref[0:tm, :].astype(BF16)

    def request_next_rows():
        for q in range(rows_per_step):
            row_copy(srcn_ref, j * rows_per_step + q).start()

    _ffn_body(valid, hb_ref, wg_ref, wu_ref, wd_ref, o_ref, tm, sub, request_next_rows)


def _ffn_call(h, wg, wu, wd, tile_expert, tile_valid, src, tm, sub, tf):
    d = wg.shape[1]
    f = wg.shape[2]
    nf = f // tf
    nt = tile_expert.shape[0]

    def jj(i, j, tv):
        return jnp.where(tv[i] > 0, j, nf - 1)

    w_specs = [
        pl.BlockSpec((1, d, tf), lambda i, j, te, tv: (te[i], 0, jj(i, j, tv))),
        pl.BlockSpec((1, d, tf), lambda i, j, te, tv: (te[i], 0, jj(i, j, tv))),
        pl.BlockSpec((1, tf, d), lambda i, j, te, tv: (te[i], jj(i, j, tv), 0)),
    ]
    out_spec = pl.BlockSpec((tm, d), lambda i, j, te, tv: (i, 0),
                            pipeline_mode=pl.Buffered(1) if src is not None else None)
    out_shape = jax.ShapeDtypeStruct((nt * tm, d), F32)
    if src is None:
        grid_spec = pltpu.PrefetchScalarGridSpec(
            num_scalar_prefetch=2, grid=(nt, nf),
            in_specs=[pl.BlockSpec((tm, d), lambda i, j, te, tv: (i, 0))] + w_specs,
            out_specs=out_spec)
        return pl.pallas_call(
            functools.partial(_ffn_dense_kernel, tm=tm, sub=sub), grid_spec=grid_spec,
            out_shape=out_shape,
            compiler_params=_cparams(("arbitrary", "arbitrary")), name="ffn_dense",
        )(tile_expert, tile_valid, h, wg, wu, wd)
    n_rows = src.shape[2]
    assert n_rows % nf == 0 and n_rows >= tm
    grid_spec = pltpu.PrefetchScalarGridSpec(
        num_scalar_prefetch=2, grid=(nt, nf),
        in_specs=[pl.BlockSpec((1, 1, n_rows), lambda i, j, te, tv: (i, 0, 0),
                               memory_space=pltpu.SMEM),
                  pl.BlockSpec((1, 1, n_rows),
                               lambda i, j, te, tv: (jnp.minimum(i + 1, nt - 1), 0, 0),
                               memory_space=pltpu.SMEM),
                  pl.BlockSpec(memory_space=pl.ANY)] + w_specs,
        out_specs=out_spec,
        scratch_shapes=[pltpu.VMEM((n_rows, d), F32), pltpu.VMEM((tm, d), BF16),
                        pltpu.SemaphoreType.DMA(())])
    return pl.pallas_call(
        functools.partial(_ffn_gather_kernel, tm=tm, sub=sub, rows_per_step=n_rows // nf),
        grid_spec=grid_spec, out_shape=out_shape,
        compiler_params=_cparams(("arbitrary", "arbitrary")), name="ffn_moe",
    )(tile_expert, tile_valid, src, src, h, wg, wu, wd)


def _conv_front_kernel(x1_ref, y_ref, moda_ref, modb_ref, g_ref, w_ref, b_ref,
                       x2_ref, glu_ref, *, d):
    x2 = x1_ref[0] + moda_ref[0][:, 2 * d:3 * d] * y_ref[0]
    x2_ref[0] = x2
    h = _adaln(x2, g_ref[...], modb_ref[0], d).astype(BF16)
    u = _dot(h, w_ref[...]) + b_ref[...]
    glu_ref[0] = u[:, 0:d] * jax.nn.sigmoid(u[:, d:2 * d])


def _conv_front_call(x1, y, moda, modb, g, w, bias, tm):
    b, s, d = x1.shape
    tile = lambda: pl.BlockSpec((1, tm, d), lambda bi, si: (bi, si, 0))
    modspec = lambda mod: _mod_spec(mod, d, lambda bi, si: bi)
    const = lambda a: pl.BlockSpec(a.shape, lambda bi, si: (0,) * a.ndim)
    return pl.pallas_call(
        functools.partial(_conv_front_kernel, d=d),
        grid=(b, s // tm),
        in_specs=[tile(), tile(), modspec(moda), modspec(modb), const(g), const(w),
                  const(bias)],
        out_specs=[tile(), tile()],
        out_shape=[jax.ShapeDtypeStruct((b, s, d), F32), jax.ShapeDtypeStruct((b, s, d), F32)],
        compiler_params=_cparams(("arbitrary", "arbitrary")),
        name="conv_front",
    )(x1, y, moda[0], modb[0], g, w, bias)


def _conv_back_kernel(glu_ref, halo_ref, x2_ref, moda_ref, modb_ref, wdw_ref, bdw_ref,
                      lng_ref, lnb_ref, w2_ref, b2_ref, g_ref, wrh_ref, wrl_ref,
                      x3_ref, h4_ref, route_ref, ext_ref, conv_ref, base_ref, sh_ref,
                      *, d, tm, rows, cols):
    bi = pl.program_id(0)
    si = pl.program_id(1)

    @pl.when(jnp.logical_and(bi == 0, si == 0))
    def _():
        base_ref[...] = jnp.zeros_like(base_ref)

    ext_ref[0:HALO, :] = jnp.where(si == 0, 0.0, halo_ref[0])
    ext_ref[HALO:HALO + tm, :] = glu_ref[0]

    off = HALO - (CONV_WIDTH - 1)
    span = sh_ref.shape[1]
    for c0 in range(0, d, cols):
        for r in range(1, SUBLANES):
            sh_ref[r - 1] = ext_ref[r:r + span, c0:c0 + cols]
        for r0 in range(0, tm, rows):
            groups = (rows // SUBLANES, SUBLANES, cols)
            acc = jnp.zeros(groups, F32) + bdw_ref[:, c0:c0 + cols][None]
            for k in range(CONV_WIDTH):
                r = (off + k) % SUBLANES
                a = r0 + off + k - r
                if r == 0:
                    tap = ext_ref[a:a + rows, c0:c0 + cols]
                else:
                    tap = sh_ref[r - 1, a:a + rows, :]
                acc = acc + wdw_ref[k, :, c0:c0 + cols][None] * tap.reshape(groups)
            conv_ref[r0:r0 + rows, c0:c0 + cols] = acc.reshape(rows, cols)

    u = conv_ref[...]
    mu = jnp.mean(u, axis=-1, keepdims=True)
    uc = u - mu
    var = jnp.mean(uc * uc, axis=-1, keepdims=True)
    u = uc * lax.rsqrt(var + EPS) * lng_ref[...] + lnb_ref[...]
    u = (u * jax.nn.sigmoid(u)).astype(BF16)
    y = _dot(u, w2_ref[...]) + b2_ref[...]
    x3 = x2_ref[0] + moda_ref[0][:, 2 * d:3 * d] * y
    x3_ref[0] = x3
    h4 = _adaln(x3, g_ref[...], modb_ref[0], d)
    h4_ref[0] = h4

    hh = h4.astype(BF16)
    hl = (h4 - hh.astype(F32)).astype(BF16)
    logits = _dot(hh, wrh_ref[...]) + _dot(hl, wrh_ref[...]) + _dot(hh, wrl_ref[...])
    lane = lax.broadcasted_iota(jnp.int32, (tm, LANES), 1).astype(F32)
    lg = jnp.where(lane < N_EXPERTS, logits, -jnp.inf)
    m1 = jnp.max(lg, axis=-1, keepdims=True)
    i1 = jnp.min(jnp.where(lg == m1, lane, float(LANES)), axis=-1, keepdims=True)
    lg2 = jnp.where(lane == i1, -jnp.inf, lg)
    m2 = jnp.max(lg2, axis=-1, keepdims=True)
    i2 = jnp.min(jnp.where(lg2 == m2, lane, float(LANES)), axis=-1, keepdims=True)
    e2 = jnp.exp(m2 - m1)
    w0 = 1.0 / (1.0 + e2)
    w1 = e2 / (1.0 + e2)
    oh0 = (lane == i1).astype(F32)
    oh1 = (lane == i2).astype(F32)
    both = oh0 + oh1
    tri = (lax.broadcasted_iota(jnp.int32, (tm, tm), 0)
           > lax.broadcasted_iota(jnp.int32, (tm, tm), 1)).astype(BF16)
    cnt = _dot(tri, both.astype(BF16)) + base_ref[...]
    r0 = jnp.sum(cnt * oh0, axis=-1, keepdims=True)
    r1 = jnp.sum(cnt * oh1, axis=-1, keepdims=True)
    base_ref[...] = base_ref[...] + jnp.sum(both, axis=0, keepdims=True)
    route = jnp.where(lane == 0.0, i1, 0.0)
    route = jnp.where(lane == 1.0, i2, route)
    route = jnp.where(lane == 2.0, w0, route)
    route = jnp.where(lane == 3.0, w1, route)
    route = jnp.where(lane == 4.0, r0, route)
    route = jnp.where(lane == 5.0, r1, route)
    route_ref[0] = route


def _conv_back_call(glu, x2, moda, modb, wdw, bdw, lng, lnb, w2, b2, g, wrh, wrl, tm):
    b, s, d = x2.shape
    tile = lambda: pl.BlockSpec((1, tm, d), lambda bi, si: (bi, si, 0))
    modspec = lambda mod: _mod_spec(mod, d, lambda bi, si: bi)
    const = lambda a: pl.BlockSpec(a.shape, lambda bi, si: (0,) * a.ndim)
    hpt = tm // HALO
    halo = pl.BlockSpec((1, HALO, d), lambda bi, si: (bi, jnp.maximum(si * hpt - 1, 0), 0))
    cols = 2 * LANES
    return pl.pallas_call(
        functools.partial(_conv_back_kernel, d=d, tm=tm, rows=64, cols=cols),
        grid=(b, s // tm),
        in_specs=[tile(), halo, tile(), modspec(moda), modspec(modb), const(wdw), const(bdw),
                  const(lng), const(lnb), const(w2), const(b2), const(g), const(wrh), const(wrl)],
        out_specs=[tile(), tile(), pl.BlockSpec((1, tm, LANES), lambda bi, si: (bi, si, 0))],
        out_shape=[jax.ShapeDtypeStruct((b, s, d), F32), jax.ShapeDtypeStruct((b, s, d), F32),
                   jax.ShapeDtypeStruct((b, s, LANES), F32)],
        scratch_shapes=[pltpu.VMEM((tm + HALO, d), F32), pltpu.VMEM((tm, d), F32),
                        pltpu.VMEM((1, LANES), F32),
                        pltpu.VMEM((SUBLANES - 1, tm + HALO - SUBLANES, cols), F32)],
        compiler_params=_cparams(("arbitrary", "arbitrary")),
        name="conv_back",
    )(glu, glu, x2, moda[0], modb[0], wdw, bdw, lng, lnb, w2, b2, g, wrh, wrl)


def _combine_kernel(p0_ref, p1_ref, p0n_ref, p1n_ref, x_ref, route_ref, mod_ref, fg_ref, ys_hbm,
                    o_ref, y_ref, sem, *, d, tm):
    i = pl.program_id(0)
    slot = i % 2

    def request(pa_ref, pb_ref, sl):
        def issue(r, c):
            pltpu.make_async_copy(ys_hbm.at[pl.ds(pa_ref[0, 0, r], 1), :],
                                  y_ref.at[sl, 0, pl.ds(r, 1), :], sem.at[sl, 0]).start()
            pltpu.make_async_copy(ys_hbm.at[pl.ds(pb_ref[0, 0, r], 1), :],
                                  y_ref.at[sl, 1, pl.ds(r, 1), :], sem.at[sl, 1]).start()
            return c
        lax.fori_loop(0, tm, issue, 0, unroll=8)

    @pl.when(i == 0)
    def _():
        request(p0_ref, p1_ref, 0)

    @pl.when(i + 1 < pl.num_programs(0))
    def _():
        request(p0n_ref, p1n_ref, 1 - slot)

    for pick in range(2):
        pltpu.make_async_copy(ys_hbm.at[pl.ds(0, tm), :], y_ref.at[slot, pick],
                              sem.at[slot, pick]).wait()
    w0 = route_ref[:, 2:3]
    w1 = route_ref[:, 3:4]
    y = w0 * y_ref[slot, 0] + w1 * y_ref[slot, 1]
    x = x_ref[...] + mod_ref[0][:, 2 * d:3 * d] * y
    o_ref[...] = _rms(x, fg_ref[...])


def _combine_call(p0, p1, x3, route, mod, fg, ys, tm, tiles_per_batch):
    t, d = x3.shape
    nt = t // tm
    nxt = lambda i: (jnp.minimum(i + 1, nt - 1), 0, 0)
    return pl.pallas_call(
        functools.partial(_combine_kernel, d=d, tm=tm),
        grid=(nt,),
        in_specs=[
            pl.BlockSpec((1, 1, tm), lambda i: (i, 0, 0), memory_space=pltpu.SMEM),
            pl.BlockSpec((1, 1, tm), lambda i: (i, 0, 0), memory_space=pltpu.SMEM),
            pl.BlockSpec((1, 1, tm), nxt, memory_space=pltpu.SMEM),
            pl.BlockSpec((1, 1, tm), nxt, memory_space=pltpu.SMEM),
            pl.BlockSpec((tm, d), lambda i: (i, 0)),
            pl.BlockSpec((tm, LANES), lambda i: (i, 0)),
            _mod_spec(mod, d, lambda i: i // tiles_per_batch),
            pl.BlockSpec(fg.shape, lambda i: (0, 0)),
            pl.BlockSpec(memory_space=pl.ANY),
        ],
        out_specs=pl.BlockSpec((tm, d), lambda i: (i, 0)),
        out_shape=jax.ShapeDtypeStruct((t, d), F32),
        scratch_shapes=[pltpu.VMEM((2, 2, tm, d), F32), pltpu.SemaphoreType.DMA((2, 2))],
        compiler_params=_cparams(("arbitrary",)),
        name="combine",
    )(p0, p1, p0, p1, x3, route, mod[0], fg, ys)


def _rope_consts():
    lane = np.arange(LANES)
    half = QK_ROPE // 2
    freq = np.where(lane < QK_ROPE, ROPE_THETA ** (-(2.0 * (lane % half)) / QK_ROPE), 0.0)
    sa = np.where((lane >= half) & (lane < QK_ROPE), 1.0, 0.0)
    sb = np.where(lane < half, -1.0, 0.0)
    rc = np.zeros((8, LANES), np.float32)
    rc[0], rc[1], rc[2] = freq, sa, sb
    return jnp.asarray(rc)


def _moe_plan(route, tm, n_tiles, n_rows):
    t = route.shape[0]
    assert n_tiles == (2 * t) // tm + N_EXPERTS and (2 * t) % tm == 0
    e0 = route[:, 0].astype(jnp.int32)
    e1 = route[:, 1].astype(jnp.int32)
    r0 = route[:, 4].astype(jnp.int32)
    r1 = route[:, 5].astype(jnp.int32)
    ids = jnp.arange(N_EXPERTS, dtype=jnp.int32)
    counts = (jnp.sum(e0[:, None] == ids[None, :], axis=0)
              + jnp.sum(e1[:, None] == ids[None, :], axis=0)).astype(jnp.int32)
    tiles = (counts + tm - 1) // tm
    tile_end = jnp.cumsum(tiles)
    tile_start = tile_end - tiles
    pos0 = tile_start[e0] * tm + r0
    pos1 = tile_start[e1] * tm + r1
    tok = jnp.arange(t, dtype=jnp.int32)
    pos = jnp.concatenate([pos0, pos1])
    src = jnp.zeros((n_tiles * n_rows,), jnp.int32).at[(pos // tm) * n_rows + pos % tm].set(
        jnp.concatenate([tok, tok]), unique_indices=True, mode="promise_in_bounds")
    ti = jnp.arange(n_tiles, dtype=jnp.int32)
    used = tile_end[-1]
    te = jnp.sum(tile_end[None, :] <= jnp.minimum(ti, used - 1)[:, None], axis=1)
    te = jnp.minimum(te, N_EXPERTS - 1).astype(jnp.int32)
    tv = jnp.clip(counts[te] - (ti - tile_start[te]) * tm, 0, tm)
    tv = jnp.where(ti < used, tv, 0).astype(jnp.int32)
    return pos0, pos1, src.reshape(n_tiles, 1, n_rows), te, tv


def kernel(x, c, positions, ada_w, ada_b, norm_g, mla_w_in, mla_g_q, mla_g_kv, mla_w_uq,
           mla_w_ukv, mla_w_out, conv_w_pw1, conv_b_pw1, conv_w_dw, conv_b_dw, conv_ln_g,
           conv_ln_b, conv_w_pw2, conv_b_pw2, ffn_w_gate, ffn_w_up, ffn_w_down, moe_w_router,
           moe_w_gate, moe_w_up, moe_w_down, final_g):
    b, s, d = x.shape
    t = b * s
    f = ffn_w_gate.shape[-1]
    assert ada_w.shape[0] == 2 and mla_w_in.shape[0] == 1 and conv_w_pw1.shape[0] == 1
    assert moe_w_gate.shape[1] == N_EXPERTS and conv_w_dw.shape[1] == CONV_WIDTH

    scale = (QK_NOPE + QK_ROPE) ** -0.5
    w_in = jnp.pad(mla_w_in[0], ((0, 0), (0, LANES - QK_ROPE))).astype(BF16)
    wq = mla_w_uq[0].reshape(Q_LORA, MLA_HEADS, QK_NOPE + QK_ROPE) * scale
    wq = jnp.pad(wq, ((0, 0), (0, 0), (0, HEAD_PAD - QK_NOPE - QK_ROPE)))
    wq = wq.reshape(Q_LORA, MLA_HEADS * HEAD_PAD).astype(BF16)
    wkv = mla_w_ukv[0].reshape(KV_LORA, MLA_HEADS, QK_NOPE + V_HEAD)
    wk = wkv[:, :, :QK_NOPE].reshape(KV_LORA, MLA_HEADS * QK_NOPE).astype(BF16)
    wv = wkv[:, :, QK_NOPE:].reshape(KV_LORA, MLA_HEADS * V_HEAD).astype(BF16)
    w_out = mla_w_out[0].astype(BF16)
    w_pw1 = conv_w_pw1[0].astype(BF16)
    w_pw2 = conv_w_pw2[0].astype(BF16)
    wr = jnp.pad(moe_w_router[0], ((0, 0), (0, LANES - N_EXPERTS)))
    wr_hi = wr.astype(BF16)
    wr_lo = (wr - wr_hi.astype(F32)).astype(BF16)
    row = lambda v: v.reshape(1, -1)

    mods = _ada_call(c, ada_w.reshape(4, d, 3 * d), ada_b.reshape(4, 3 * d))
    mods = mods.reshape(4, b, 1, 3 * d)
    mod = [(mods, i) for i in range(4)]

    q, k, v = _mla_front_call(x, positions.reshape(b, s, 1), mod[0], row(norm_g[0, 0]), w_in,
                              row(mla_g_q[0]), row(mla_g_kv[0]), wq, wk, wv, _rope_consts(),
                              tm=256)
    o = _attn_call(q, k, v, t=512, heads=8)
    x1, h2 = _attn_out_call(x, o, w_out, mod[0], mod[1], row(norm_g[0, 1]), tm=512)
    tm_d = 1024
    nt_d = t // tm_d
    y1 = _ffn_call(h2.reshape(t, d), ffn_w_gate.astype(BF16), ffn_w_up.astype(BF16),
                   ffn_w_down.astype(BF16), jnp.zeros((nt_d,), jnp.int32),
                   jnp.full((nt_d,), tm_d, jnp.int32), None, tm=tm_d, sub=tm_d, tf=512)

    x2, glu = _conv_front_call(x1, y1.reshape(b, s, d), mod[1], mod[2], row(norm_g[1, 0]),
                               w_pw1, row(conv_b_pw1[0]), tm=256)
    w_dw8 = jnp.broadcast_to(conv_w_dw[0][:, None, :], (CONV_WIDTH, SUBLANES, d))
    b_dw8 = jnp.broadcast_to(row(conv_b_dw[0]), (SUBLANES, d))
    x3, h4, route = _conv_back_call(glu, x2, mod[2], mod[3], w_dw8, b_dw8,
                                    row(conv_ln_g[0]), row(conv_ln_b[0]), w_pw2,
                                    row(conv_b_pw2[0]), row(norm_g[1, 1]), wr_hi, wr_lo, tm=256)
    route = route.reshape(t, LANES)
    tm_e = 1024
    tf_e = 512
    nt_e = (2 * t) // tm_e + N_EXPERTS
    nf_e = f // tf_e
    rps = -(-tm_e // nf_e)
    while (rps * nf_e) % SUBLANES:
        rps += 1
    rows_e = rps * nf_e
    pos0, pos1, src, te, tv = _moe_plan(route, tm_e, nt_e, rows_e)
    ys = _ffn_call(h4.reshape(t, d), moe_w_gate[0], moe_w_up[0], moe_w_down[0], te, tv, src,
                   tm=tm_e, sub=128, tf=tf_e)
    tm_c = 256
    out = _combine_call(pos0.reshape(t // tm_c, 1, tm_c), pos1.reshape(t // tm_c, 1, tm_c),
                        x3.reshape(t, d), route, mod[3], row(final_g), ys, tm=tm_c,
                        tiles_per_batch=s // tm_c)
    return out.reshape(b, s, d)
```
